```python
import jax, jax.numpy as jnp
from jax import lax
import numpy as np

D_MODEL = 1024
BATCH = 8
SEQ = 4096
DEPTH = 4

N_HEADS = 16
HEAD_DIM = D_MODEL // N_HEADS
D_FF = ((8 * D_MODEL // 3 + 127) // 128) * 128
CONV_WIDTH = 3
Q_BLOCK = 128
N_MIXERS = 2
N_MOD = 6
RMS_EPS = 1e-6
N_SB_LAYERS = (DEPTH + 1) // 2
N_FOX_LAYERS = DEPTH // 2

kernel_name = "hybrid_stickbreak_fox_convffn"


def rmsnorm(x, g):
    xf = x.astype(jnp.float32)
    y = xf * lax.rsqrt(jnp.mean(xf * xf, axis=-1, keepdims=True) + RMS_EPS)
    return (y * g.astype(jnp.float32)).astype(x.dtype)


def modulate(h, shift, scale):
    return h * (1.0 + scale[:, None, :]) + shift[:, None, :]


def split_heads(t):
    b, s, _ = t.shape
    return t.reshape(b, s, N_HEADS, HEAD_DIM).transpose(0, 2, 1, 3)


def merge_heads(t):
    b, h, s, d = t.shape
    return t.transpose(0, 2, 1, 3).reshape(b, s, h * d)


def stick_breaking_attention(q, k, v):
    s_len = q.shape[2]
    scale = HEAD_DIM ** -0.5
    outs = []
    for blk in range(s_len // Q_BLOCK):
        q0 = blk * Q_BLOCK
        k_end = q0 + Q_BLOCK
        qb, kb, vb = q[:, :, q0:k_end], k[:, :, :k_end], v[:, :, :k_end]
        z = jnp.einsum('bhqd,bhkd->bhqk', qb, kb).astype(jnp.float32) * scale
        t_idx = q0 + jnp.arange(Q_BLOCK)[:, None]
        s_idx = jnp.arange(k_end)[None, :]
        strict = s_idx < t_idx
        log_beta = jax.nn.log_sigmoid(z)
        log_1mb = jnp.where(strict, jax.nn.log_sigmoid(-z), 0.0)
        after = lax.cumsum(log_1mb, axis=3, reverse=True) - log_1mb
        a = jnp.where(strict, jnp.exp(log_beta + after), 0.0)
        outs.append(jnp.einsum('bhqk,bhkd->bhqd', a.astype(vb.dtype), vb))
    return jnp.concatenate(outs, axis=2)


def forgetting_attention(q, k, v, log_f):
    s_len = q.shape[2]
    scale = HEAD_DIM ** -0.5
    cum = lax.cumsum(log_f, axis=2)
    outs = []
    for blk in range(s_len // Q_BLOCK):
        q0 = blk * Q_BLOCK
        k_end = q0 + Q_BLOCK
        qb, kb, vb = q[:, :, q0:k_end], k[:, :, :k_end], v[:, :, :k_end]
        z = jnp.einsum('bhqd,bhkd->bhqk', qb, kb).astype(jnp.float32) * scale
        decay = cum[:, :, q0:k_end, None] - cum[:, :, None, :k_end]
        t_idx = q0 + jnp.arange(Q_BLOCK)[:, None]
        s_idx = jnp.arange(k_end)[None, :]
        logits = jnp.where(s_idx <= t_idx, z + decay, -jnp.inf)
        p = jax.nn.softmax(logits, axis=-1)
        outs.append(jnp.einsum('bhqk,bhkd->bhqd', p.astype(vb.dtype), vb))
    return jnp.concatenate(outs, axis=2)


def causal_depthwise_conv(h, w, b):
    s_len = h.shape[1]
    hp = jnp.pad(h, ((0, 0), (CONV_WIDTH - 1, 0), (0, 0)))
    out = b[None, None, :]
    for kk in range(CONV_WIDTH):
        out = out + w[kk][None, None, :] * hp[:, kk:kk + s_len]
    return out


def setup_inputs(seed: int = 0) -> dict:
    key = jax.random.key(seed)
    ks = jax.random.split(key, 20)
    f32 = jnp.float32
    d, f, h = D_MODEL, D_FF, N_HEADS

    def nrm(k, shape, s):
        return jax.random.normal(k, shape, f32) * s

    return {
        "x": nrm(ks[0], (BATCH, SEQ, d), 1.0),
        "c": nrm(ks[1], (BATCH, d), 1.0),
        "w_mod": nrm(ks[2], (DEPTH, d, N_MOD * d), 0.5 * d ** -0.5),
        "b_mod": nrm(ks[3], (DEPTH, N_MOD * d), 0.02),
        "g_mix_pre": 1.0 + nrm(ks[4], (DEPTH, d), 0.05),
        "g_mix_post": 1.0 + nrm(ks[5], (DEPTH, d), 0.05),
        "w_qkv": nrm(ks[6], (DEPTH, d, 3 * d), d ** -0.5),
        "w_o": nrm(ks[7], (DEPTH, d, d), d ** -0.5),
        "w_fg": nrm(ks[8], (N_FOX_LAYERS, d, h), d ** -0.5),
        "b_fg": 3.0 + nrm(ks[9], (N_FOX_LAYERS, h), 0.5),
        "g_ffn_pre": 1.0 + nrm(ks[10], (DEPTH, d), 0.05),
        "g_ffn_post": 1.0 + nrm(ks[11], (DEPTH, d), 0.05),
        "w_ffn_gate": nrm(ks[12], (DEPTH, d, f), d ** -0.5),
        "w_ffn_up": nrm(ks[13], (DEPTH, d, f), d ** -0.5),
        "w_conv": nrm(ks[14], (DEPTH, CONV_WIDTH, f), CONV_WIDTH ** -0.5),
        "b_conv": nrm(ks[15], (DEPTH, f), 0.02),
        "w_ffn_down": nrm(ks[16], (DEPTH, f, d), f ** -0.5),
    }


def reference(x, c, w_mod, b_mod, g_mix_pre, g_mix_post, w_qkv, w_o, w_fg, b_fg,
              g_ffn_pre, g_ffn_post, w_ffn_gate, w_ffn_up, w_conv, b_conv, w_ffn_down):
    c_act = jax.nn.silu(c)
    for i in range(DEPTH):
        mod = jnp.einsum('bd,dm->bm', c_act, w_mod[i]) + b_mod[i]
        sh_a, sc_a, gt_a, sh_f, sc_f, gt_f = jnp.split(mod, N_MOD, axis=-1)

        h = modulate(rmsnorm(x, g_mix_pre[i]), sh_a, sc_a)
        qkv = jnp.einsum('bsd,de->bse', h, w_qkv[i])
        q, k, v = (split_heads(t) for t in jnp.split(qkv, 3, axis=-1))
        if i % N_MIXERS == 0:
            o = stick_breaking_attention(q, k, v)
        else:
            j = i // N_MIXERS
            f_logit = jnp.einsum('bsd,dh->bhs', h, w_fg[j]) + b_fg[j][None, :, None]
            log_f = jax.nn.log_sigmoid(f_logit.astype(jnp.float32))
            o = forgetting_attention(q, k, v, log_f)
        o = jnp.einsum('bse,ed->bsd', merge_heads(o), w_o[i])
        x = x + gt_a[:, None, :] * rmsnorm(o, g_mix_post[i])

        h = modulate(rmsnorm(x, g_ffn_pre[i]), sh_f, sc_f)
        gate = jnp.einsum('bsd,df->bsf', h, w_ffn_gate[i])
        up = jnp.einsum('bsd,df->bsf', h, w_ffn_up[i])
        gate = causal_depthwise_conv(gate, w_conv[i], b_conv[i])
        y = jnp.einsum('bsf,fd->bsd', jax.nn.silu(gate) * up, w_ffn_down[i])
        x = x + gt_f[:, None, :] * rmsnorm(y, g_ffn_post[i])
    return x
```

```python
import functools

import jax
import jax.numpy as jnp
from jax import lax
from jax.experimental import pallas as pl
from jax.experimental.pallas import tpu as pltpu

F32 = jnp.float32
BF16 = jnp.bfloat16

N_HEADS = 16
HEAD_DIM = 64
N_MOD = 6
RMS_EPS = 1e-6
LANES = 128
SUBLANES = 8
HEADS_PER_BLOCK = LANES // HEAD_DIM
VMEM_LIMIT_BYTES = 56 * 1024 * 1024
NEG_BIG = -1e30

ATTN_TILE = 256
ROW_TILE = 512
FFN_ROW_TILE = 256


def _cparams(*sem):
    return pltpu.CompilerParams(dimension_semantics=sem, vmem_limit_bytes=VMEM_LIMIT_BYTES)


def _split3(x):
    p1 = x.astype(BF16)
    r1 = x - p1.astype(F32)
    p2 = r1.astype(BF16)
    r2 = r1 - p2.astype(F32)
    return p1, p2, r2.astype(BF16)


def _dot(a, b):
    return jnp.dot(a, b, preferred_element_type=F32)


def _rms(x):
    return x * lax.rsqrt(jnp.mean(x * x, axis=-1, keepdims=True) + RMS_EPS)


def _mod_kernel(c_ref, w_ref, b_ref, o_ref):
    c = c_ref[...]
    ca = c / (1.0 + jnp.exp(-c))
    c1, c2, _ = _split3(ca)
    w = w_ref[0]
    w1 = w.astype(BF16)
    w2 = (w - w1.astype(F32)).astype(BF16)
    o_ref[0] = _dot(c1, w1) + _dot(c2, w1) + _dot(c1, w2) + b_ref[0]


def _modulation(c, w_mod, b_mod):
    depth, d, m = w_mod.shape
    b = c.shape[0]
    tn = 1536
    return pl.pallas_call(
        _mod_kernel,
        grid=(depth, m // tn),
        in_specs=[
            pl.BlockSpec((b, d), lambda i, j: (0, 0)),
            pl.BlockSpec((1, d, tn), lambda i, j: (i, 0, j)),
            pl.BlockSpec((1, 1, tn), lambda i, j: (i, 0, j)),
        ],
        out_specs=pl.BlockSpec((1, b, tn), lambda i, j: (i, 0, j)),
        out_shape=jax.ShapeDtypeStruct((depth, b, m), F32),
        compiler_params=_cparams("arbitrary", "arbitrary"),
        name="adaln_mod",
    )(c, w_mod, b_mod.reshape(depth, 1, m))


def _qkv_kernel(x_ref, g_ref, sh_ref, sc_ref, w_ref, *rest, d, q_scale, fox):
    if fox:
        wfg_ref, bfg_ref, q_ref, k_ref, v_ref, cum_ref, carry_ref = rest
    else:
        q_ref, k_ref, v_ref = rest
    x = x_ref[0]
    h = (_rms(x) * g_ref[...]) * (1.0 + sc_ref[0]) + sh_ref[0]
    hb = h.astype(BF16)
    q_ref[0] = (_dot(hb, w_ref[:, 0:d]) * q_scale).astype(BF16)
    k_ref[0] = _dot(hb, w_ref[:, d:2 * d]).astype(BF16)
    v_ref[0] = _dot(hb, w_ref[:, 2 * d:3 * d]).astype(BF16)
    if fox:
        tm = x.shape[0]

        @pl.when(pl.program_id(1) == 0)
        def _():
            carry_ref[...] = jnp.zeros_like(carry_ref)

        fl = _dot(hb, wfg_ref[...]) + bfg_ref[...]
        lf = jnp.minimum(fl, 0.0) - jnp.log(1.0 + jnp.exp(-jnp.abs(fl)))
        row = lax.broadcasted_iota(jnp.int32, (tm, tm), 0)
        col = lax.broadcasted_iota(jnp.int32, (tm, tm), 1)
        tri = jnp.where(row >= col, 1.0, 0.0).astype(BF16)
        p1, p2, p3 = _split3(lf)
        cs = _dot(tri, p1) + _dot(tri, p2) + _dot(tri, p3) + carry_ref[0:1, :]
        cum_ref[0] = cs
        carry_ref[...] = jnp.broadcast_to(cs[tm - 1:tm, :], carry_ref.shape)


def _qkv(x, g, shift, scale, w_qkv, q_scale, w_fg=None, b_fg=None):
    b, s, d = x.shape
    tm = ROW_TILE
    fox = w_fg is not None
    row_spec = pl.BlockSpec((1, tm, d), lambda i, j: (i, j, 0))
    vec_spec = pl.BlockSpec((1, d), lambda i, j: (0, 0))
    mod_spec = pl.BlockSpec((1, 1, d), lambda i, j: (i, 0, 0))
    in_specs = [row_spec, vec_spec, mod_spec, mod_spec,
                pl.BlockSpec((d, 3 * d), lambda i, j: (0, 0))]
    args = [x, g.reshape(1, d), shift.reshape(b, 1, d), scale.reshape(b, 1, d), w_qkv]
    out_specs = [row_spec, row_spec, row_spec]
    out_shape = [jax.ShapeDtypeStruct((b, s, d), BF16)] * 3
    scratch = []
    if fox:
        in_specs += [pl.BlockSpec((d, LANES), lambda i, j: (0, 0)),
                     pl.BlockSpec((1, LANES), lambda i, j: (0, 0))]
        args += [w_fg, b_fg]
        out_specs.append(pl.BlockSpec((1, tm, LANES), lambda i, j: (i, j, 0)))
        out_shape.append(jax.ShapeDtypeStruct((b, s, LANES), F32))
        scratch.append(pltpu.VMEM((SUBLANES, LANES), F32))
    return pl.pallas_call(
        functools.partial(_qkv_kernel, d=d, q_scale=q_scale, fox=fox),
        grid=(b, s // tm),
        in_specs=in_specs,
        out_specs=out_specs,
        out_shape=out_shape,
        scratch_shapes=scratch,
        compiler_params=_cparams("arbitrary", "arbitrary"),
        name="norm_qkv_fox" if fox else "norm_qkv",
    )(*args)


def _sublane_suffix_excl_prod(tot):
    n = SUBLANES
    rid = lax.broadcasted_iota(jnp.int32, tot.shape, 0)
    x = jnp.where(rid < n - 1, pltpu.roll(tot, n - 1, 0), 1.0)
    y = x * jnp.where(rid < n - 1, pltpu.roll(x, n - 1, 0), 1.0)
    y = y * jnp.where(rid < n - 2, pltpu.roll(y, n - 2, 0), 1.0)
    y = y * jnp.where(rid < n - 4, pltpu.roll(y, n - 4, 0), 1.0)
    return y


def _qk(kt, qh):
    return lax.dot_general(kt, qh, (((1,), (1,)), ((), ())), preferred_element_type=F32)


def _head_masks(q):
    lane = lax.broadcasted_iota(jnp.int32, q.shape, 1)
    zero = jnp.zeros_like(q)
    return [jnp.where(lane < HEAD_DIM, q, zero), jnp.where(lane >= HEAD_DIM, q, zero)]


def _sb_kernel(q_ref, k_ref, v_ref, o_ref, kp_ref, vt_ref, tmp_ref, *, t):
    s = q_ref.shape[1]
    nt = s // t
    m_rows = t // SUBLANES

    def prep(i, carry):
        base = pl.multiple_of(i * t, t)
        for r in range(SUBLANES):
            tmp_ref[pl.ds(r, m_rows, stride=SUBLANES), :] = (
                k_ref[0, pl.ds(base + r * m_rows, m_rows), :].astype(F32))
        kp_ref[i] = tmp_ref[...].astype(BF16)
        for r in range(SUBLANES):
            tmp_ref[pl.ds(r, m_rows, stride=SUBLANES), :] = (
                v_ref[0, pl.ds(base + r * m_rows, m_rows), :].astype(F32))
        vt_ref[i] = tmp_ref[...].T.astype(BF16)
        return carry

    lax.fori_loop(0, nt, prep, 0)

    rho = lax.broadcasted_iota(jnp.int32, (t, t), 0)
    key_in_tile = (rho & (SUBLANES - 1)) * m_rows + (rho >> 3)
    qry_in_tile = lax.broadcasted_iota(jnp.int32, (t, t), 1)
    strict = key_in_tile < qry_in_tile

    def step(qh, kt, vt, acc, carry_p, mask):
        hb = 0.5 * jnp.tanh(_qk(kt, qh))
        beta = 0.5 + hb
        omb = 0.5 - hb
        if mask is not None:
            beta = jnp.where(mask, beta, 0.0)
            omb = jnp.where(mask, omb, 1.0)
        run = jnp.ones((SUBLANES, t), F32)
        slabs = [None] * m_rows
        for m in reversed(range(m_rows)):
            lo = m * SUBLANES
            slabs[m] = beta[lo:lo + SUBLANES, :] * run
            run = run * omb[lo:lo + SUBLANES, :]
        off = _sublane_suffix_excl_prod(run) * carry_p
        a = jnp.concatenate([sl * off for sl in slabs], axis=0).astype(BF16)
        acc = acc + _dot(vt, a)
        total = (off * run)[0:1, :]
        return acc, jnp.broadcast_to(total, (SUBLANES, t))

    def q_body(qi, carry):
        q0 = pl.multiple_of(qi * t, t)
        qhs = _head_masks(q_ref[0, pl.ds(q0, t), :])
        outs = []
        for h in range(HEADS_PER_BLOCK):
            rows = pl.ds(h * HEAD_DIM, HEAD_DIM)
            acc0 = jnp.zeros((HEAD_DIM, t), F32)
            p0 = jnp.ones((SUBLANES, t), F32)
            acc, p = step(qhs[h], kp_ref[qi], vt_ref[qi, rows, :], acc0, p0, strict)

            def k_body(i, c, h=h, rows=rows):
                kj = qi - 1 - i
                return step(qhs[h], kp_ref[kj], vt_ref[kj, rows, :], c[0], c[1], None)

            acc, p = lax.fori_loop(0, qi, k_body, (acc, p))
            outs.append(acc)
        o_ref[0, pl.ds(q0, t), :] = jnp.concatenate(outs, axis=0).T.astype(BF16)
        return carry

    lax.fori_loop(0, nt, q_body, 0)


def _fox_kernel(q_ref, k_ref, v_ref, cum_ref, o_ref, vt_ref, bias_ref, *, t):
    s = q_ref.shape[1]
    nt = s // t
    hp = pl.program_id(1)

    def prep(i, carry):
        base = pl.multiple_of(i * t, t)
        vt_ref[i] = v_ref[0, pl.ds(base, t), :].astype(F32).T.astype(BF16)
        cum = cum_ref[0, pl.ds(base, t), :]
        lane = lax.broadcasted_iota(jnp.int32, cum.shape, 1)
        for h in range(HEADS_PER_BLOCK):
            col = jnp.sum(jnp.where(lane == hp * HEADS_PER_BLOCK + h, cum, 0.0),
                          axis=1, keepdims=True)
            bias_ref[h, i] = jnp.broadcast_to(col, (t, LANES))
        return carry

    lax.fori_loop(0, nt, prep, 0)

    key_in_tile = lax.broadcasted_iota(jnp.int32, (t, t), 0)
    qry_in_tile = lax.broadcasted_iota(jnp.int32, (t, t), 1)
    causal = key_in_tile <= qry_in_tile
    rep = t // LANES
    hrep = HEAD_DIM // SUBLANES

    def step(qh, kt, vt, bias, acc, m_run, l_run, mask):
        z = _qk(kt, qh) - jnp.concatenate([bias] * rep, axis=1)
        if mask is not None:
            z = jnp.where(mask, z, NEG_BIG)
        m_new = jnp.maximum(m_run, jnp.max(z, axis=0, keepdims=True))
        p = jnp.exp(z - m_new)
        alpha = jnp.exp(m_run - m_new)
        l_new = alpha * l_run + jnp.sum(p, axis=0, keepdims=True)
        acc = acc * alpha + _dot(vt, p.astype(BF16))
        return acc, m_new, l_new

    def q_body(qi, carry):
        q0 = pl.multiple_of(qi * t, t)
        qhs = _head_masks(q_ref[0, pl.ds(q0, t), :])
        outs = []
        for h in range(HEADS_PER_BLOCK):
            rows = pl.ds(h * HEAD_DIM, HEAD_DIM)
            init = (jnp.zeros((HEAD_DIM, t), F32), jnp.full((1, t), NEG_BIG, F32),
                    jnp.zeros((1, t), F32))
            kt = k_ref[0, pl.ds(q0, t), :]
            c = step(qhs[h], kt, vt_ref[qi, rows, :], bias_ref[h, qi], *init, causal)

            def k_body(i, c, h=h, rows=rows):
                k0 = pl.multiple_of(i * t, t)
                return step(qhs[h], k_ref[0, pl.ds(k0, t), :], vt_ref[i, rows, :],
                            bias_ref[h, i], *c, None)

            acc, _, l_run = lax.fori_loop(0, qi, k_body, c)
            outs.append(acc / l_run)
        o_ref[0, pl.ds(q0, t), :] = jnp.concatenate(outs, axis=0).T.astype(BF16)
        return carry

    lax.fori_loop(0, nt, q_body, 0)


def _attention(q, k, v, cum=None):
    b, s, d = q.shape
    t = ATTN_TILE
    nt = s // t
    blk = pl.BlockSpec((1, s, LANES), lambda i, j: (i, 0, j))
    in_specs = [blk, blk, blk]
    args = [q, k, v]
    if cum is None:
        body = functools.partial(_sb_kernel, t=t)
        scratch = [pltpu.VMEM((nt, t, LANES), BF16), pltpu.VMEM((nt, LANES, t), BF16),
                   pltpu.VMEM((t, LANES), F32)]
        name = "stickbreak_attn"
    else:
        body = functools.partial(_fox_kernel, t=t)
        in_specs.append(pl.BlockSpec((1, s, LANES), lambda i, j: (i, 0, 0)))
        args.append(cum)
        scratch = [pltpu.VMEM((nt, LANES, t), BF16),
                   pltpu.VMEM((HEADS_PER_BLOCK, nt, t, LANES), F32)]
        name = "forgetting_attn"
    return pl.pallas_call(
        body,
        grid=(b, d // LANES),
        in_specs=in_specs,
        out_specs=blk,
        out_shape=jax.ShapeDtypeStruct((b, s, d), BF16),
        scratch_shapes=scratch,
        compiler_params=_cparams("arbitrary", "arbitrary"),
        name=name,
    )(*args)


def _proj_res_kernel(a_ref, w_ref, x_ref, g_ref, gt_ref, o_ref):
    y = _dot(a_ref[0], w_ref[...])
    o_ref[0] = x_ref[0] + gt_ref[0] * (_rms(y) * g_ref[...])


def _proj_res(a, w, x, g, gate):
    b, s, d = x.shape
    kdim = a.shape[-1]
    tm = ROW_TILE
    return pl.pallas_call(
        _proj_res_kernel,
        grid=(b, s // tm),
        in_specs=[
            pl.BlockSpec((1, tm, kdim), lambda i, j: (i, j, 0)),
            pl.BlockSpec((kdim, d), lambda i, j: (0, 0)),
            pl.BlockSpec((1, tm, d), lambda i, j: (i, j, 0)),
            pl.BlockSpec((1, d), lambda i, j: (0, 0)),
            pl.BlockSpec((1, 1, d), lambda i, j: (i, 0, 0)),
        ],
        out_specs=pl.BlockSpec((1, tm, d), lambda i, j: (i, j, 0)),
        out_shape=jax.ShapeDtypeStruct((b, s, d), F32),
        compiler_params=_cparams("arbitrary", "arbitrary"),
        name="proj_norm_residual",
    )(a, w, x, g.reshape(1, d), gate.reshape(b, 1, d))


def _ffn_up_kernel(x_ref, g_ref, sh_ref, sc_ref, wg_ref, wu_ref, wc_ref, bc_ref, o_ref, gbuf_ref):
    tm = x_ref.shape[1]
    halo = SUBLANES

    @pl.when(pl.program_id(1) == 0)
    def _():
        gbuf_ref[0:halo, :] = jnp.zeros((halo, gbuf_ref.shape[1]), F32)

    x = x_ref[0]
    h = ((_rms(x) * g_ref[...]) * (1.0 + sc_ref[0]) + sh_ref[0]).astype(BF16)
    gate = _dot(h, wg_ref[...])
    gbuf_ref[halo:halo + tm, :] = gate
    g1 = gbuf_ref[halo - 1:halo - 1 + tm, :]
    g2 = gbuf_ref[halo - 2:halo - 2 + tm, :]
    cg = bc_ref[...] + wc_ref[0:1, :] * g2 + wc_ref[1:2, :] * g1 + wc_ref[2:3, :] * gate
    act = cg / (1.0 + jnp.exp(-cg))
    up = _dot(h, wu_ref[...])
    o_ref[0] = (act * up).astype(BF16)
    gbuf_ref[0:halo, :] = gbuf_ref[tm:tm + halo, :]


def _ffn_up(x, g, shift, scale, w_gate, w_up, w_conv, b_conv):
    b, s, d = x.shape
    f = w_gate.shape[1]
    tm = FFN_ROW_TILE
    cw = w_conv.shape[0]
    mod_spec = pl.BlockSpec((1, 1, d), lambda i, j: (i, 0, 0))
    return pl.pallas_call(
        _ffn_up_kernel,
        grid=(b, s // tm),
        in_specs=[
            pl.BlockSpec((1, tm, d), lambda i, j: (i, j, 0)),
            pl.BlockSpec((1, d), lambda i, j: (0, 0)),
            mod_spec, mod_spec,
            pl.BlockSpec((d, f), lambda i, j: (0, 0)),
            pl.BlockSpec((d, f), lambda i, j: (0, 0)),
            pl.BlockSpec((cw, f), lambda i, j: (0, 0)),
            pl.BlockSpec((1, f), lambda i, j: (0, 0)),
        ],
        out_specs=pl.BlockSpec((1, tm, f), lambda i, j: (i, j, 0)),
        out_shape=jax.ShapeDtypeStruct((b, s, f), BF16),
        scratch_shapes=[pltpu.VMEM((tm + SUBLANES, f), F32)],
        compiler_params=_cparams("arbitrary", "arbitrary"),
        name="ffn_gate_up_conv",
    )(x, g.reshape(1, d), shift.reshape(b, 1, d), scale.reshape(b, 1, d),
      w_gate, w_up, w_conv, b_conv.reshape(1, f))


def kernel(x, c, w_mod, b_mod, g_mix_pre, g_mix_post, w_qkv, w_o, w_fg, b_fg,
           g_ffn_pre, g_ffn_post, w_ffn_gate, w_ffn_up, w_conv, b_conv, w_ffn_down):
    depth = w_mod.shape[0]
    d = x.shape[-1]
    n_heads = w_fg.shape[-1]
    assert d == N_HEADS * HEAD_DIM and n_heads == N_HEADS and w_conv.shape[1] == 3
    scale = HEAD_DIM ** -0.5

    mod = _modulation(c, w_mod, b_mod)
    w_qkv_b = w_qkv.astype(BF16)
    w_o_b = w_o.astype(BF16)
    w_gate_b = w_ffn_gate.astype(BF16)
    w_up_b = w_ffn_up.astype(BF16)
    w_down_b = w_ffn_down.astype(BF16)
    w_fg_b = jnp.pad(w_fg, ((0, 0), (0, 0), (0, LANES - n_heads))).astype(BF16)
    b_fg_p = jnp.pad(b_fg, ((0, 0), (0, LANES - n_heads))).reshape(-1, 1, LANES)

    for i in range(depth):
        sh_a, sc_a, gt_a, sh_f, sc_f, gt_f = [mod[i, :, j * d:(j + 1) * d] for j in range(N_MOD)]
        if i % 2 == 0:
            q, k, v = _qkv(x, g_mix_pre[i], sh_a, sc_a, w_qkv_b[i], 0.5 * scale)
            o = _attention(q, k, v)
        else:
            j = i // 2
            q, k, v, cum = _qkv(x, g_mix_pre[i], sh_a, sc_a, w_qkv_b[i], scale,
                                w_fg_b[j], b_fg_p[j])
            o = _attention(q, k, v, cum)
        x = _proj_res(o, w_o_b[i], x, g_mix_post[i], gt_a)
        act = _ffn_up(x, g_ffn_pre[i], sh_f, sc_f, w_gate_b[i], w_up_b[i], w_conv[i], b_conv[i])
        x = _proj_res(act, w_down_b[i], x, g_ffn_post[i], gt_f)
    return x
```

```python
import functools

import jax
import jax.numpy as jnp
from jax import lax
from jax.experimental import pallas as pl
from jax.experimental.pallas import tpu as pltpu

F32 = jnp.float32
BF16 = jnp.bfloat16

N_HEADS = 16
HEAD_DIM = 64
N_MOD = 6
RMS_EPS = 1e-6
LANES = 128
SUBLANES = 8
HEADS_PER_LANE_TILE = LANES // HEAD_DIM
VMEM_LIMIT_BYTES = 56 * 1024 * 1024
NEG_BIG = -1e30

ATTN_TILE = 256
ATTN_LANE_TILES = 2
ROW_TILE = 512
FFN_ROW_TILE = 256


def _cparams(*sem):
    return pltpu.CompilerParams(dimension_semantics=sem, vmem_limit_bytes=VMEM_LIMIT_BYTES)


def _split3(x):
    p1 = x.astype(BF16)
    r1 = x - p1.astype(F32)
    p2 = r1.astype(BF16)
    r2 = r1 - p2.astype(F32)
    return p1, p2, r2.astype(BF16)


def _dot(a, b):
    return jnp.dot(a, b, preferred_element_type=F32)


def _rms(x):
    return x * lax.rsqrt(jnp.mean(x * x, axis=-1, keepdims=True) + RMS_EPS)


def _mod_kernel(c_ref, w_ref, b_ref, o_ref):
    c = c_ref[...]
    ca = c / (1.0 + jnp.exp(-c))
    c1, c2, _ = _split3(ca)
    w = w_ref[0]
    w1 = w.astype(BF16)
    w2 = (w - w1.astype(F32)).astype(BF16)
    o_ref[0] = _dot(c1, w1) + _dot(c2, w1) + _dot(c1, w2) + b_ref[0]


def _modulation(c, w_mod, b_mod):
    depth, d, m = w_mod.shape
    b = c.shape[0]
    tn = 1536
    return pl.pallas_call(
        _mod_kernel,
        grid=(depth, m // tn),
        in_specs=[
            pl.BlockSpec((b, d), lambda i, j: (0, 0)),
            pl.BlockSpec((1, d, tn), lambda i, j: (i, 0, j)),
            pl.BlockSpec((1, 1, tn), lambda i, j: (i, 0, j)),
        ],
        out_specs=pl.BlockSpec((1, b, tn), lambda i, j: (i, 0, j)),
        out_shape=jax.ShapeDtypeStruct((depth, b, m), F32),
        compiler_params=_cparams("arbitrary", "arbitrary"),
        name="adaln_mod",
    )(c, w_mod, b_mod.reshape(depth, 1, m))


def _qkv_kernel(x_ref, g_ref, sh_ref, sc_ref, w_ref, *rest, d, q_scale, fox):
    if fox:
        wfg_ref, bfg_ref, q_ref, k_ref, v_ref, cum_ref, carry_ref = rest
    else:
        q_ref, k_ref, v_ref = rest
    x = x_ref[0]
    h = (_rms(x) * g_ref[...]) * (1.0 + sc_ref[0]) + sh_ref[0]
    hb = h.astype(BF16)
    q_ref[0] = (_dot(hb, w_ref[:, 0:d]) * q_scale).astype(BF16)
    k_ref[0] = _dot(hb, w_ref[:, d:2 * d]).astype(BF16)
    v_ref[0] = _dot(hb, w_ref[:, 2 * d:3 * d]).astype(BF16)
    if fox:
        tm = x.shape[0]

        @pl.when(pl.program_id(1) == 0)
        def _():
            carry_ref[...] = jnp.zeros_like(carry_ref)

        fl = _dot(hb, wfg_ref[...]) + bfg_ref[...]
        lf = jnp.minimum(fl, 0.0) - jnp.log(1.0 + jnp.exp(-jnp.abs(fl)))
        row = lax.broadcasted_iota(jnp.int32, (tm, tm), 0)
        col = lax.broadcasted_iota(jnp.int32, (tm, tm), 1)
        tri = jnp.where(row >= col, 1.0, 0.0).astype(BF16)
        p1, p2, p3 = _split3(lf)
        cs = _dot(tri, p1) + _dot(tri, p2) + _dot(tri, p3) + carry_ref[0:1, :]
        cum_ref[0] = cs
        carry_ref[...] = jnp.broadcast_to(cs[tm - 1:tm, :], carry_ref.shape)


def _qkv(x, g, shift, scale, w_qkv, q_scale, w_fg=None, b_fg=None):
    b, s, d = x.shape
    tm = ROW_TILE
    fox = w_fg is not None
    row_spec = pl.BlockSpec((1, tm, d), lambda i, j: (i, j, 0))
    vec_spec = pl.BlockSpec((1, d), lambda i, j: (0, 0))
    mod_spec = pl.BlockSpec((1, 1, d), lambda i, j: (i, 0, 0))
    in_specs = [row_spec, vec_spec, mod_spec, mod_spec,
                pl.BlockSpec((d, 3 * d), lambda i, j: (0, 0))]
    args = [x, g.reshape(1, d), shift.reshape(b, 1, d), scale.reshape(b, 1, d), w_qkv]
    out_specs = [row_spec, row_spec, row_spec]
    out_shape = [jax.ShapeDtypeStruct((b, s, d), BF16)] * 3
    scratch = []
    if fox:
        in_specs += [pl.BlockSpec((d, LANES), lambda i, j: (0, 0)),
                     pl.BlockSpec((1, LANES), lambda i, j: (0, 0))]
        args += [w_fg, b_fg]
        out_specs.append(pl.BlockSpec((1, tm, LANES), lambda i, j: (i, j, 0)))
        out_shape.append(jax.ShapeDtypeStruct((b, s, LANES), F32))
        scratch.append(pltpu.VMEM((SUBLANES, LANES), F32))
    return pl.pallas_call(
        functools.partial(_qkv_kernel, d=d, q_scale=q_scale, fox=fox),
        grid=(b, s // tm),
        in_specs=in_specs,
        out_specs=out_specs,
        out_shape=out_shape,
        scratch_shapes=scratch,
        compiler_params=_cparams("arbitrary", "arbitrary"),
        name="norm_qkv_fox" if fox else "norm_qkv",
    )(*args)


def _sublane_suffix_excl_prod(tot):
    n = SUBLANES
    rid = lax.broadcasted_iota(jnp.int32, tot.shape, 0)
    x = jnp.where(rid < n - 1, pltpu.roll(tot, n - 1, 0), 1.0)
    y = x * jnp.where(rid < n - 1, pltpu.roll(x, n - 1, 0), 1.0)
    y = y * jnp.where(rid < n - 2, pltpu.roll(y, n - 2, 0), 1.0)
    y = y * jnp.where(rid < n - 4, pltpu.roll(y, n - 4, 0), 1.0)
    return y


def _store_head_queries(qt_ref, g0, i, q_tile):
    qt = q_tile.astype(F32).T
    row = lax.broadcasted_iota(jnp.int32, qt.shape, 0)
    for h in range(HEADS_PER_LANE_TILE):
        mine = (row >= h * HEAD_DIM) & (row < (h + 1) * HEAD_DIM)
        qt_ref[g0 + h, i] = jnp.where(mine, qt, 0.0).astype(BF16)


def _lane_tile(p):
    return pl.ds(p * LANES, LANES)


def _head_rows(g):
    return pl.ds((g % HEADS_PER_LANE_TILE) * HEAD_DIM, HEAD_DIM)


def _advance(qi, kj):
    last = kj == 0
    return jnp.where(last, qi + 1, qi), jnp.where(last, qi + 1, kj - 1)


def _walk_tiles(nt, first_step, later_step, finish, state0):
    n_steps = nt * (nt + 1) // 2
    zero = jnp.int32(0)

    def clamp(p):
        return jnp.minimum(p[0], nt - 1), jnp.minimum(p[1], nt - 1)

    cur = (zero, zero)
    nxt = _advance(*cur)
    state = first_step(None, cur, clamp(nxt), state0)

    def body(_, c):
        prev, cur, state = (c[0], c[1]), (c[2], c[3]), c[4]
        nxt = _advance(*cur)
        state = lax.cond(cur[1] == cur[0],
                         lambda st: first_step(prev, cur, clamp(nxt), st),
                         lambda st: later_step(prev, cur, clamp(nxt), st), state)
        return (*cur, *nxt, state)

    c = lax.fori_loop(1, n_steps, body, (*cur, *nxt, state))
    finish((c[0], c[1]), c[4])


def _sb_kernel(q_ref, k_ref, v_ref, o_ref, kp_ref, vt_ref, tmp_ref, qt_ref, z_ref, w_ref, a_ref,
               acc_ref, *, t):
    s = q_ref.shape[1]
    nt = s // t
    m_rows = t // SUBLANES
    lane_tiles = range(q_ref.shape[2] // LANES)
    heads = range(len(lane_tiles) * HEADS_PER_LANE_TILE)

    def prep(i, carry):
        base = pl.multiple_of(i * t, t)
        for p in lane_tiles:
            for r in range(SUBLANES):
                tmp_ref[pl.ds(r, m_rows, stride=SUBLANES), :] = (
                    k_ref[0, pl.ds(base + r * m_rows, m_rows), _lane_tile(p)].astype(F32))
            kp_ref[p, i] = tmp_ref[...].astype(BF16)
            for r in range(SUBLANES):
                tmp_ref[pl.ds(r, m_rows, stride=SUBLANES), :] = (
                    v_ref[0, pl.ds(base + r * m_rows, m_rows), _lane_tile(p)].astype(F32))
            vt_ref[p, i] = tmp_ref[...].T.astype(BF16)
            _store_head_queries(qt_ref, p * HEADS_PER_LANE_TILE, i,
                                q_ref[0, pl.ds(base, t), _lane_tile(p)])
        return carry

    lax.fori_loop(0, nt, prep, 0)

    rho = lax.broadcasted_iota(jnp.int32, (t, t), 0)
    key_in_tile = (rho & (SUBLANES - 1)) * m_rows + (rho >> 3)
    qry_in_tile = lax.broadcasted_iota(jnp.int32, (t, t), 1)
    strict = key_in_tile < qry_in_tile
    ones = jnp.ones((SUBLANES, t), F32)
    reread = jnp.minimum(pl.program_id(0), 0)

    def scores(pos):
        return [_dot(kp_ref[g // HEADS_PER_LANE_TILE, pos[1]], qt_ref[g, pos[0]]) for g in heads]

    def values(pos):
        return jnp.concatenate(
            [_dot(vt_ref[g // HEADS_PER_LANE_TILE, pos[1], _head_rows(g), :], a_ref[g])
             for g in heads], axis=0)

    def weights(g, carry_p, mask):
        omb = 0.5 - 0.5 * jnp.tanh(z_ref[g])
        if mask is not None:
            omb = jnp.where(mask, omb, 1.0)
        w_ref[g] = omb
        parts = [omb[m * SUBLANES:(m + 1) * SUBLANES, :] for m in range(m_rows)]
        while len(parts) > 1:
            parts = [parts[i] * parts[i + 1] for i in range(0, len(parts), 2)]
        run = _sublane_suffix_excl_prod(parts[0]) * carry_p
        pack = 2 * SUBLANES
        for j in reversed(range(t // pack)):
            w = w_ref[g + reread, pl.ds(j * pack, pack), :]
            mid = run * w[SUBLANES:, :]
            low = mid * w[:SUBLANES, :]
            a_ref[g, pl.ds(j * pack, pack), :] = jnp.concatenate(
                [mid - low, run - mid], axis=0).astype(BF16)
            run = low
        return jnp.broadcast_to(run[0:1, :], (SUBLANES, t))

    def write_out(pos, out):
        rows = pl.ds(pl.multiple_of(pos[0] * t, t), t)
        for p in lane_tiles:
            o_ref[0, rows, _lane_tile(p)] = out[p * LANES:(p + 1) * LANES, :].T.astype(BF16)

    def first_step(prev, cur, nxt, state):
        if prev is None:
            zs = scores(cur)
            for g in heads:
                z_ref[g] = zs[g]
        zs = scores(nxt)
        if prev is not None:
            write_out(prev, acc_ref[...] + values(prev))
        acc_ref[...] = jnp.zeros_like(acc_ref)
        state = tuple(weights(g, ones, strict) for g in heads)
        for g in heads:
            z_ref[g] = zs[g]
        return state

    def later_step(prev, cur, nxt, state):
        zs = scores(nxt)
        acc_ref[...] += values(prev)
        state = tuple(weights(g, state[g], None) for g in heads)
        for g in heads:
            z_ref[g] = zs[g]
        return state

    def finish(prev, state):
        write_out(prev, acc_ref[...] + values(prev))

    _walk_tiles(nt, first_step, later_step, finish, tuple(ones for _ in heads))


def _fox_kernel(q_ref, k_ref, v_ref, cum_ref, o_ref, vt_ref, bias_ref, qt_ref, z_ref, w_ref, p_ref,
                acc_ref, *, t):
    s = q_ref.shape[1]
    nt = s // t
    lane_tiles = range(q_ref.shape[2] // LANES)
    heads = range(len(lane_tiles) * HEADS_PER_LANE_TILE)
    head0 = pl.program_id(1) * len(heads)

    def prep(i, carry):
        base = pl.multiple_of(i * t, t)
        for p in lane_tiles:
            vt_ref[p, i] = v_ref[0, pl.ds(base, t), _lane_tile(p)].astype(F32).T.astype(BF16)
            _store_head_queries(qt_ref, p * HEADS_PER_LANE_TILE, i,
                                q_ref[0, pl.ds(base, t), _lane_tile(p)])
        cum = cum_ref[0, pl.ds(base, t), :]
        lane = lax.broadcasted_iota(jnp.int32, cum.shape, 1)
        for g in heads:
            col = jnp.sum(jnp.where(lane == head0 + g, cum, 0.0), axis=1, keepdims=True)
            bias_ref[g, i] = jnp.broadcast_to(col, (t, LANES))
        return carry

    lax.fori_loop(0, nt, prep, 0)

    key_in_tile = lax.broadcasted_iota(jnp.int32, (t, t), 0)
    qry_in_tile = lax.broadcasted_iota(jnp.int32, (t, t), 1)
    causal = key_in_tile <= qry_in_tile
    rep = t // LANES
    fresh = (jnp.full((1, t), NEG_BIG, F32), jnp.zeros((1, t), F32), jnp.zeros((1, t), F32))
    reread = jnp.minimum(pl.program_id(0), 0)

    def scores(pos):
        rows = pl.ds(pl.multiple_of(pos[1] * t, t), t)
        return [_dot(k_ref[0, rows, _lane_tile(g // HEADS_PER_LANE_TILE)], qt_ref[g, pos[0]])
                for g in heads]

    def values(pos):
        return [_dot(vt_ref[g // HEADS_PER_LANE_TILE, pos[1], _head_rows(g), :], p_ref[g])
                for g in heads]

    def weights(g, pos, st, mask):
        m_run, l_run, _ = st
        z = z_ref[g] - jnp.concatenate([bias_ref[g, pos[1]]] * rep, axis=1)
        if mask is not None:
            z = jnp.where(mask, z, NEG_BIG)
        w_ref[g] = z
        m_new = jnp.maximum(m_run, jnp.max(z, axis=0, keepdims=True))
        p = jnp.exp(w_ref[g + reread] - m_new)
        alpha = jnp.exp(m_run - m_new)
        p_ref[g] = p.astype(BF16)
        return m_new, alpha * l_run + jnp.sum(p, axis=0, keepdims=True), alpha

    def rescaled(g, av, st):
        return acc_ref[pl.ds(g * HEAD_DIM, HEAD_DIM), :] * st[2] + av[g]

    def write_out(pos, av, state):
        rows = pl.ds(pl.multiple_of(pos[0] * t, t), t)
        for p in lane_tiles:
            out = jnp.concatenate(
                [rescaled(g, av, state[g]) / state[g][1]
                 for g in range(p * HEADS_PER_LANE_TILE, (p + 1) * HEADS_PER_LANE_TILE)], axis=0)
            o_ref[0, rows, _lane_tile(p)] = out.T.astype(BF16)

    def first_step(prev, cur, nxt, state):
        if prev is None:
            zs = scores(cur)
            for g in heads:
                z_ref[g] = zs[g]
        zs = scores(nxt)
        if prev is not None:
            write_out(prev, values(prev), state)
        acc_ref[...] = jnp.zeros_like(acc_ref)
        state = tuple(weights(g, cur, fresh, causal) for g in heads)
        for g in heads:
            z_ref[g] = zs[g]
        return state

    def later_step(prev, cur, nxt, state):
        zs = scores(nxt)
        av = values(prev)
        for g in heads:
            acc_ref[pl.ds(g * HEAD_DIM, HEAD_DIM), :] = rescaled(g, av, state[g])
        state = tuple(weights(g, cur, state[g], None) for g in heads)
        for g in heads:
            z_ref[g] = zs[g]
        return state

    def finish(prev, state):
        write_out(prev, values(prev), state)

    _walk_tiles(nt, first_step, later_step, finish, tuple(fresh for _ in heads))


def _attention(q, k, v, cum=None):
    b, s, d = q.shape
    t = ATTN_TILE
    nt = s // t
    n_lt = ATTN_LANE_TILES
    n_heads = n_lt * HEADS_PER_LANE_TILE
    width = n_lt * LANES
    blk = pl.BlockSpec((1, s, width), lambda i, j: (i, 0, j))
    in_specs = [blk, blk, blk]
    args = [q, k, v]
    pipe = [pltpu.VMEM((n_heads, nt, LANES, t), BF16),
            pltpu.VMEM((n_heads, t, t), F32), pltpu.VMEM((n_heads, t, t), F32),
            pltpu.VMEM((n_heads, t, t), BF16), pltpu.VMEM((width, t), F32)]
    if cum is None:
        body = functools.partial(_sb_kernel, t=t)
        scratch = [pltpu.VMEM((n_lt, nt, t, LANES), BF16), pltpu.VMEM((n_lt, nt, LANES, t), BF16),
                   pltpu.VMEM((t, LANES), F32)] + pipe
        name = "stickbreak_attn"
    else:
        body = functools.partial(_fox_kernel, t=t)
        in_specs.append(pl.BlockSpec((1, s, LANES), lambda i, j: (i, 0, 0)))
        args.append(cum)
        scratch = [pltpu.VMEM((n_lt, nt, LANES, t), BF16),
                   pltpu.VMEM((n_heads, nt, t, LANES), F32)] + pipe
        name = "forgetting_attn"
    return pl.pallas_call(
        body,
        grid=(b, d // width),
        in_specs=in_specs,
        out_specs=blk,
        out_shape=jax.ShapeDtypeStruct((b, s, d), BF16),
        scratch_shapes=scratch,
        compiler_params=_cparams("arbitrary", "arbitrary"),
        name=name,
    )(*args)


def _proj_res_kernel(a_ref, w_ref, x_ref, g_ref, gt_ref, o_ref):
    y = _dot(a_ref[0], w_ref[...])
    o_ref[0] = x_ref[0] + gt_ref[0] * (_rms(y) * g_ref[...])


def _proj_res(a, w, x, g, gate):
    b, s, d = x.shape
    kdim = a.shape[-1]
    tm = ROW_TILE
    return pl.pallas_call(
        _proj_res_kernel,
        grid=(b, s // tm),
        in_specs=[
            pl.BlockSpec((1, tm, kdim), lambda i, j: (i, j, 0)),
            pl.BlockSpec((kdim, d), lambda i, j: (0, 0)),
            pl.BlockSpec((1, tm, d), lambda i, j: (i, j, 0)),
            pl.BlockSpec((1, d), lambda i, j: (0, 0)),
            pl.BlockSpec((1, 1, d), lambda i, j: (i, 0, 0)),
        ],
        out_specs=pl.BlockSpec((1, tm, d), lambda i, j: (i, j, 0)),
        out_shape=jax.ShapeDtypeStruct((b, s, d), F32),
        compiler_params=_cparams("arbitrary", "arbitrary"),
        name="proj_norm_residual",
    )(a, w, x, g.reshape(1, d), gate.reshape(b, 1, d))


def _ffn_up_kernel(x_ref, g_ref, sh_ref, sc_ref, wg_ref, wu_ref, wc_ref, bc_ref, o_ref, gbuf_ref):
    tm = x_ref.shape[1]
    halo = SUBLANES

    @pl.when(pl.program_id(1) == 0)
    def _():
        gbuf_ref[0:halo, :] = jnp.zeros((halo, gbuf_ref.shape[1]), F32)

    x = x_ref[0]
    h = ((_rms(x) * g_ref[...]) * (1.0 + sc_ref[0]) + sh_ref[0]).astype(BF16)
    gate = _dot(h, wg_ref[...])
    gbuf_ref[halo:halo + tm, :] = gate
    g1 = gbuf_ref[halo - 1:halo - 1 + tm, :]
    g2 = gbuf_ref[halo - 2:halo - 2 + tm, :]
    cg = bc_ref[...] + wc_ref[0:1, :] * g2 + wc_ref[1:2, :] * g1 + wc_ref[2:3, :] * gate
    act = cg / (1.0 + jnp.exp(-cg))
    up = _dot(h, wu_ref[...])
    o_ref[0] = (act * up).astype(BF16)
    gbuf_ref[0:halo, :] = gbuf_ref[tm:tm + halo, :]


def _ffn_up(x, g, shift, scale, w_gate, w_up, w_conv, b_conv):
    b, s, d = x.shape
    f = w_gate.shape[1]
    tm = FFN_ROW_TILE
    cw = w_conv.shape[0]
    mod_spec = pl.BlockSpec((1, 1, d), lambda i, j: (i, 0, 0))
    return pl.pallas_call(
        _ffn_up_kernel,
        grid=(b, s // tm),
        in_specs=[
            pl.BlockSpec((1, tm, d), lambda i, j: (i, j, 0)),
            pl.BlockSpec((1, d), lambda i, j: (0, 0)),
            mod_spec, mod_spec,
            pl.BlockSpec((d, f), lambda i, j: (0, 0)),
            pl.BlockSpec((d, f), lambda i, j: (0, 0)),
            pl.BlockSpec((cw, f), lambda i, j: (0, 0)),
            pl.BlockSpec((1, f), lambda i, j: (0, 0)),
        ],
        out_specs=pl.BlockSpec((1, tm, f), lambda i, j: (i, j, 0)),
        out_shape=jax.ShapeDtypeStruct((b, s, f), BF16),
        scratch_shapes=[pltpu.VMEM((tm + SUBLANES, f), F32)],
        compiler_params=_cparams("arbitrary", "arbitrary"),
        name="ffn_gate_up_conv",
    )(x, g.reshape(1, d), shift.reshape(b, 1, d), scale.reshape(b, 1, d),
      w_gate, w_up, w_conv, b_conv.reshape(1, f))


def kernel(x, c, w_mod, b_mod, g_mix_pre, g_mix_post, w_qkv, w_o, w_fg, b_fg,
           g_ffn_pre, g_ffn_post, w_ffn_gate, w_ffn_up, w_conv, b_conv, w_ffn_down):
    depth = w_mod.shape[0]
    d = x.shape[-1]
    n_heads = w_fg.shape[-1]
    assert d == N_HEADS * HEAD_DIM and n_heads == N_HEADS and w_conv.shape[1] == 3
    scale = HEAD_DIM ** -0.5

    mod = _modulation(c, w_mod, b_mod)
    w_qkv_b = w_qkv.astype(BF16)
    w_o_b = w_o.astype(BF16)
    w_gate_b = w_ffn_gate.astype(BF16)
    w_up_b = w_ffn_up.astype(BF16)
    w_down_b = w_ffn_down.astype(BF16)
    w_fg_b = jnp.pad(w_fg, ((0, 0), (0, 0), (0, LANES - n_heads))).astype(BF16)
    b_fg_p = jnp.pad(b_fg, ((0, 0), (0, LANES - n_heads))).reshape(-1, 1, LANES)

    for i in range(depth):
        sh_a, sc_a, gt_a, sh_f, sc_f, gt_f = [mod[i, :, j * d:(j + 1) * d] for j in range(N_MOD)]
        if i % 2 == 0:
            q, k, v = _qkv(x, g_mix_pre[i], sh_a, sc_a, w_qkv_b[i], 0.5 * scale)
            o = _attention(q, k, v)
        else:
            j = i // 2
            q, k, v, cum = _qkv(x, g_mix_pre[i], sh_a, sc_a, w_qkv_b[i], scale,
                                w_fg_b[j], b_fg_p[j])
            o = _attention(q, k, v, cum)
        x = _proj_res(o, w_o_b[i], x, g_mix_post[i], gt_a)
        act = _ffn_up(x, g_ffn_pre[i], sh_f, sc_f, w_gate_b[i], w_up_b[i], w_conv[i], b_conv[i])
        x = _proj_res(act, w_down_b[i], x, g_ffn_post[i], gt_f)
    return x
```

```python
import functools

import jax
import jax.numpy as jnp
from jax import lax
from jax.experimental import pallas as pl
from jax.experimental.pallas import tpu as pltpu

F32 = jnp.float32
BF16 = jnp.bfloat16

N_HEADS = 16
HEAD_DIM = 64
N_MOD = 6
RMS_EPS = 1e-6
LANES = 128
SUBLANES = 8
HEADS_PER_LANE_TILE = LANES // HEAD_DIM
VMEM_LIMIT_BYTES = 56 * 1024 * 1024
NEG_BIG = -1e30

ATTN_TILE = 256
SB_LANE_TILES = 2
FOX_LANE_TILES = 2
ROW_TILE = 512
FFN_ROW_TILE = 256


def _cparams(*sem):
    return pltpu.CompilerParams(dimension_semantics=sem, vmem_limit_bytes=VMEM_LIMIT_BYTES)


def _split3(x):
    p1 = x.astype(BF16)
    r1 = x - p1.astype(F32)
    p2 = r1.astype(BF16)
    r2 = r1 - p2.astype(F32)
    return p1, p2, r2.astype(BF16)


def _dot(a, b):
    return jnp.dot(a, b, preferred_element_type=F32)


def _rms(x):
    return x * lax.rsqrt(jnp.mean(x * x, axis=-1, keepdims=True) + RMS_EPS)


def _mod_kernel(c_ref, w_ref, b_ref, o_ref):
    c = c_ref[...]
    ca = c / (1.0 + jnp.exp(-c))
    c1, c2, _ = _split3(ca)
    w = w_ref[0]
    w1 = w.astype(BF16)
    w2 = (w - w1.astype(F32)).astype(BF16)
    o_ref[0] = _dot(c1, w1) + _dot(c2, w1) + _dot(c1, w2) + b_ref[0]


def _modulation(c, w_mod, b_mod):
    depth, d, m = w_mod.shape
    b = c.shape[0]
    tn = 1536
    return pl.pallas_call(
        _mod_kernel,
        grid=(depth, m // tn),
        in_specs=[
            pl.BlockSpec((b, d), lambda i, j: (0, 0)),
            pl.BlockSpec((1, d, tn), lambda i, j: (i, 0, j)),
            pl.BlockSpec((1, 1, tn), lambda i, j: (i, 0, j)),
        ],
        out_specs=pl.BlockSpec((1, b, tn), lambda i, j: (i, 0, j)),
        out_shape=jax.ShapeDtypeStruct((depth, b, m), F32),
        compiler_params=_cparams("arbitrary", "arbitrary"),
        name="adaln_mod",
    )(c, w_mod, b_mod.reshape(depth, 1, m))


def _qkv_kernel(x_ref, g_ref, sh_ref, sc_ref, w_ref, *rest, d, q_scale, fox):
    if fox:
        wfg_ref, bfg_ref, q_ref, k_ref, v_ref, cum_ref, carry_ref = rest
    else:
        q_ref, k_ref, v_ref = rest
    x = x_ref[0]
    h = (_rms(x) * g_ref[...]) * (1.0 + sc_ref[0]) + sh_ref[0]
    hb = h.astype(BF16)
    q_ref[0] = (_dot(hb, w_ref[:, 0:d]) * q_scale).astype(BF16)
    k_ref[0] = _dot(hb, w_ref[:, d:2 * d]).astype(BF16)
    v_ref[0] = _dot(hb, w_ref[:, 2 * d:3 * d]).astype(BF16)
    if fox:
        tm = x.shape[0]

        @pl.when(pl.program_id(1) == 0)
        def _():
            carry_ref[...] = jnp.zeros_like(carry_ref)

        fl = _dot(hb, wfg_ref[...]) + bfg_ref[...]
        lf = jnp.minimum(fl, 0.0) - jnp.log(1.0 + jnp.exp(-jnp.abs(fl)))
        row = lax.broadcasted_iota(jnp.int32, (tm, tm), 0)
        col = lax.broadcasted_iota(jnp.int32, (tm, tm), 1)
        tri = jnp.where(row >= col, 1.0, 0.0).astype(BF16)
        p1, p2, p3 = _split3(lf)
        cs = _dot(tri, p1) + _dot(tri, p2) + _dot(tri, p3) + carry_ref[0:1, :]
        cum_ref[0] = cs
        carry_ref[...] = jnp.broadcast_to(cs[tm - 1:tm, :], carry_ref.shape)


def _qkv(x, g, shift, scale, w_qkv, q_scale, w_fg=None, b_fg=None):
    b, s, d = x.shape
    tm = ROW_TILE
    fox = w_fg is not None
    row_spec = pl.BlockSpec((1, tm, d), lambda i, j: (i, j, 0))
    vec_spec = pl.BlockSpec((1, d), lambda i, j: (0, 0))
    mod_spec = pl.BlockSpec((1, 1, d), lambda i, j: (i, 0, 0))
    in_specs = [row_spec, vec_spec, mod_spec, mod_spec,
                pl.BlockSpec((d, 3 * d), lambda i, j: (0, 0))]
    args = [x, g.reshape(1, d), shift.reshape(b, 1, d), scale.reshape(b, 1, d), w_qkv]
    out_specs = [row_spec, row_spec, row_spec]
    out_shape = [jax.ShapeDtypeStruct((b, s, d), BF16)] * 3
    scratch = []
    if fox:
        in_specs += [pl.BlockSpec((d, LANES), lambda i, j: (0, 0)),
                     pl.BlockSpec((1, LANES), lambda i, j: (0, 0))]
        args += [w_fg, b_fg]
        out_specs.append(pl.BlockSpec((1, tm, LANES), lambda i, j: (i, j, 0)))
        out_shape.append(jax.ShapeDtypeStruct((b, s, LANES), F32))
        scratch.append(pltpu.VMEM((SUBLANES, LANES), F32))
    return pl.pallas_call(
        functools.partial(_qkv_kernel, d=d, q_scale=q_scale, fox=fox),
        grid=(b, s // tm),
        in_specs=in_specs,
        out_specs=out_specs,
        out_shape=out_shape,
        scratch_shapes=scratch,
        compiler_params=_cparams("arbitrary", "arbitrary"),
        name="norm_qkv_fox" if fox else "norm_qkv",
    )(*args)


def _sublane_suffix_excl_prod(tot):
    n = SUBLANES
    rid = lax.broadcasted_iota(jnp.int32, tot.shape, 0)
    x = jnp.where(rid < n - 1, pltpu.roll(tot, n - 1, 0), 1.0)
    y = x * jnp.where(rid < n - 1, pltpu.roll(x, n - 1, 0), 1.0)
    y = y * jnp.where(rid < n - 2, pltpu.roll(y, n - 2, 0), 1.0)
    y = y * jnp.where(rid < n - 4, pltpu.roll(y, n - 4, 0), 1.0)
    return y


def _store_head_queries(qt_ref, g0, i, q_tile):
    qt = q_tile.astype(F32).T
    row = lax.broadcasted_iota(jnp.int32, qt.shape, 0)
    for h in range(HEADS_PER_LANE_TILE):
        mine = (row >= h * HEAD_DIM) & (row < (h + 1) * HEAD_DIM)
        qt_ref[g0 + h, i] = jnp.where(mine, qt, 0.0).astype(BF16)


def _lane_tile(p):
    return pl.ds(p * LANES, LANES)


def _head_rows(g):
    return pl.ds((g % HEADS_PER_LANE_TILE) * HEAD_DIM, HEAD_DIM)


def _advance(qi, kj):
    last = kj == 0
    return jnp.where(last, qi + 1, qi), jnp.where(last, qi + 1, kj - 1)


def _walk_tiles(nt, step, finish, state0, masks_next=True):
    n_steps = nt * (nt + 1) // 2
    zero = jnp.int32(0)

    def clamp(p):
        return jnp.minimum(p[0], nt - 1), jnp.minimum(p[1], nt - 1)

    cur = (zero, zero)
    nxt = _advance(*cur)
    state = step(None, cur, clamp(nxt), state0, True, True)

    def body(_, c):
        prev, cur, state = (c[0], c[1]), (c[2], c[3]), c[4]
        nxt = _advance(*cur)
        nxt_c = clamp(nxt)
        cur_first = cur[1] == cur[0]
        nxt_first = nxt_c[1] == nxt_c[0]
        branches = [lambda st: step(prev, cur, nxt_c, st, True, False),
                    lambda st: step(prev, cur, nxt_c, st, False, False),
                    lambda st: step(prev, cur, nxt_c, st, False, True)]
        case = jnp.where(cur_first, 0, jnp.where(nxt_first, 2, 1) if masks_next else 1)
        state = lax.switch(case, branches if masks_next else branches[:2], state)
        return (*cur, *nxt, state)

    c = lax.fori_loop(1, n_steps, body, (*cur, *nxt, state))
    finish((c[0], c[1]), c[4])


def _sb_kernel(q_ref, k_ref, v_ref, o_ref, kp_ref, vt_ref, tmp_ref, qt_ref, w_ref, a_ref, acc_ref,
               *, t):
    s = q_ref.shape[1]
    nt = s // t
    m_rows = t // SUBLANES
    lane_tiles = range(q_ref.shape[2] // LANES)
    heads = range(len(lane_tiles) * HEADS_PER_LANE_TILE)

    def prep(i, carry):
        base = pl.multiple_of(i * t, t)
        for p in lane_tiles:
            for r in range(SUBLANES):
                tmp_ref[pl.ds(r, m_rows, stride=SUBLANES), :] = (
                    k_ref[0, pl.ds(base + r * m_rows, m_rows), _lane_tile(p)].astype(F32))
            kp_ref[p, i] = tmp_ref[...].astype(BF16)
            for r in range(SUBLANES):
                tmp_ref[pl.ds(r, m_rows, stride=SUBLANES), :] = (
                    v_ref[0, pl.ds(base + r * m_rows, m_rows), _lane_tile(p)].astype(F32))
            vt_ref[p, i] = tmp_ref[...].T.astype(BF16)
            _store_head_queries(qt_ref, p * HEADS_PER_LANE_TILE, i,
                                q_ref[0, pl.ds(base, t), _lane_tile(p)])
        return carry

    lax.fori_loop(0, nt, prep, 0)

    rho = lax.broadcasted_iota(jnp.int32, (t, t), 0)
    key_in_tile = (rho & (SUBLANES - 1)) * m_rows + (rho >> 3)
    qry_in_tile = lax.broadcasted_iota(jnp.int32, (t, t), 1)
    strict = key_in_tile < qry_in_tile
    ones = jnp.ones((SUBLANES, t), F32)

    def scores(pos):
        return [_dot(kp_ref[g // HEADS_PER_LANE_TILE, pos[1]], qt_ref[g, pos[0]]) for g in heads]

    def values(pos):
        return jnp.concatenate(
            [_dot(vt_ref[g // HEADS_PER_LANE_TILE, pos[1], _head_rows(g), :], a_ref[g])
             for g in heads], axis=0)

    def complements(g, z, masked):
        omb = 0.5 - 0.5 * jnp.tanh(z)
        if masked:
            omb = jnp.where(strict, omb, 1.0)
        w_ref[g] = omb
        seg = omb[0:SUBLANES, :]
        for m in range(1, m_rows):
            seg = seg * omb[m * SUBLANES:(m + 1) * SUBLANES, :]
        return seg

    def scan(g, carry_p, seg):
        run = _sublane_suffix_excl_prod(seg) * carry_p
        pack = 2 * SUBLANES
        for j in reversed(range(t // pack)):
            w = w_ref[g, pl.ds(j * pack, pack), :]
            mid = run * w[SUBLANES:, :]
            low = mid * w[:SUBLANES, :]
            a_ref[g, pl.ds(j * pack, pack), :] = jnp.concatenate(
                [mid - low, run - mid], axis=0).astype(BF16)
            run = low
        return jnp.broadcast_to(run[0:1, :], (SUBLANES, t))

    def write_out(pos, out):
        rows = pl.ds(pl.multiple_of(pos[0] * t, t), t)
        for p in lane_tiles:
            o_ref[0, rows, _lane_tile(p)] = out[p * LANES:(p + 1) * LANES, :].T.astype(BF16)

    def step(prev, cur, nxt, state, cur_first, nxt_first):
        if prev is None:
            zs = scores(cur)
            state = tuple((ones, complements(g, zs[g], True)) for g in heads)
        zs = scores(nxt)
        if prev is not None:
            if cur_first:
                write_out(prev, acc_ref[...] + values(prev))
            else:
                acc_ref[...] += values(prev)
        if cur_first:
            acc_ref[...] = jnp.zeros_like(acc_ref)
        carries = [scan(g, ones if cur_first else state[g][0], state[g][1]) for g in heads]
        return tuple((carries[g], complements(g, zs[g], nxt_first)) for g in heads)

    def finish(prev, state):
        write_out(prev, acc_ref[...] + values(prev))

    _walk_tiles(nt, step, finish, None)


def _fox_kernel(q_ref, k_ref, v_ref, cum_ref, o_ref, vt_ref, bias_ref, qt_ref, z_ref, w_ref, p_ref,
                acc_ref, *, t):
    s = q_ref.shape[1]
    nt = s // t
    lane_tiles = range(q_ref.shape[2] // LANES)
    heads = range(len(lane_tiles) * HEADS_PER_LANE_TILE)
    head0 = pl.program_id(1) * len(heads)

    def prep(i, carry):
        base = pl.multiple_of(i * t, t)
        for p in lane_tiles:
            vt_ref[p, i] = v_ref[0, pl.ds(base, t), _lane_tile(p)].astype(F32).T.astype(BF16)
            _store_head_queries(qt_ref, p * HEADS_PER_LANE_TILE, i,
                                q_ref[0, pl.ds(base, t), _lane_tile(p)])
        cum = cum_ref[0, pl.ds(base, t), :]
        lane = lax.broadcasted_iota(jnp.int32, cum.shape, 1)
        for g in heads:
            col = jnp.sum(jnp.where(lane == head0 + g, cum, 0.0), axis=1, keepdims=True)
            bias_ref[g, i] = jnp.broadcast_to(col, (t, LANES))
        return carry

    lax.fori_loop(0, nt, prep, 0)

    key_in_tile = lax.broadcasted_iota(jnp.int32, (t, t), 0)
    qry_in_tile = lax.broadcasted_iota(jnp.int32, (t, t), 1)
    causal = key_in_tile <= qry_in_tile
    rep = t // LANES
    neg = jnp.full((1, t), NEG_BIG, F32)
    zero = jnp.zeros((1, t), F32)
    reread = jnp.minimum(pl.program_id(0), 0)

    def scores(pos):
        rows = pl.ds(pl.multiple_of(pos[1] * t, t), t)
        return [_dot(k_ref[0, rows, _lane_tile(g // HEADS_PER_LANE_TILE)], qt_ref[g, pos[0]])
                for g in heads]

    def values(pos):
        return [_dot(vt_ref[g // HEADS_PER_LANE_TILE, pos[1], _head_rows(g), :], p_ref[g])
                for g in heads]

    def weights(g, pos, m_run, l_run, masked):
        z = z_ref[g] - jnp.concatenate([bias_ref[g, pos[1]]] * rep, axis=1)
        if masked:
            z = jnp.where(causal, z, NEG_BIG)
        w_ref[g] = z
        m_new = jnp.maximum(m_run, jnp.max(z, axis=0, keepdims=True))
        p = jnp.exp(w_ref[g + reread] - m_new)
        alpha = jnp.exp(m_run - m_new)
        p_ref[g] = p.astype(BF16)
        return m_new, alpha * l_run + jnp.sum(p, axis=0, keepdims=True), alpha

    def rescaled(g, av, alpha):
        return acc_ref[pl.ds(g * HEAD_DIM, HEAD_DIM), :] * alpha + av[g]

    def write_out(pos, av, state):
        rows = pl.ds(pl.multiple_of(pos[0] * t, t), t)
        for p in lane_tiles:
            out = jnp.concatenate(
                [rescaled(g, av, state[g][2]) / state[g][1]
                 for g in range(p * HEADS_PER_LANE_TILE, (p + 1) * HEADS_PER_LANE_TILE)], axis=0)
            o_ref[0, rows, _lane_tile(p)] = out.T.astype(BF16)

    def step(prev, cur, nxt, state, cur_first, nxt_first):
        if prev is None:
            zs = scores(cur)
            for g in heads:
                z_ref[g] = zs[g]
        zs = scores(nxt)
        if prev is not None:
            av = values(prev)
            if cur_first:
                write_out(prev, av, state)
            else:
                for g in heads:
                    acc_ref[pl.ds(g * HEAD_DIM, HEAD_DIM), :] = rescaled(g, av, state[g][2])
        if cur_first:
            acc_ref[...] = jnp.zeros_like(acc_ref)
        state = tuple(weights(g, cur, neg if cur_first else state[g][0],
                              zero if cur_first else state[g][1], cur_first) for g in heads)
        for g in heads:
            z_ref[g] = zs[g]
        return state

    def finish(prev, state):
        write_out(prev, values(prev), state)

    _walk_tiles(nt, step, finish, None, masks_next=False)


def _attention(q, k, v, cum=None):
    b, s, d = q.shape
    t = ATTN_TILE
    nt = s // t
    n_lt = SB_LANE_TILES if cum is None else FOX_LANE_TILES
    n_heads = n_lt * HEADS_PER_LANE_TILE
    width = n_lt * LANES
    blk = pl.BlockSpec((1, s, width), lambda i, j: (i, 0, j))
    in_specs = [blk, blk, blk]
    args = [q, k, v]
    pipe = [pltpu.VMEM((n_heads, nt, LANES, t), BF16), pltpu.VMEM((n_heads, t, t), F32),
            pltpu.VMEM((n_heads, t, t), BF16), pltpu.VMEM((width, t), F32)]
    if cum is None:
        body = functools.partial(_sb_kernel, t=t)
        scratch = [pltpu.VMEM((n_lt, nt, t, LANES), BF16), pltpu.VMEM((n_lt, nt, LANES, t), BF16),
                   pltpu.VMEM((t, LANES), F32)] + pipe
        name = "stickbreak_attn"
    else:
        body = functools.partial(_fox_kernel, t=t)
        in_specs.append(pl.BlockSpec((1, s, LANES), lambda i, j: (i, 0, 0)))
        args.append(cum)
        scratch = [pltpu.VMEM((n_lt, nt, LANES, t), BF16),
                   pltpu.VMEM((n_heads, nt, t, LANES), F32), pipe[0],
                   pltpu.VMEM((n_heads, t, t), F32)] + pipe[1:]
        name = "forgetting_attn"
    return pl.pallas_call(
        body,
        grid=(b, d // width),
        in_specs=in_specs,
        out_specs=blk,
        out_shape=jax.ShapeDtypeStruct((b, s, d), BF16),
        scratch_shapes=scratch,
        compiler_params=_cparams("arbitrary", "arbitrary"),
        name=name,
    )(*args)


def _proj_res_kernel(a_ref, w_ref, x_ref, g_ref, gt_ref, o_ref):
    y = _dot(a_ref[0], w_ref[...])
    o_ref[0] = x_ref[0] + gt_ref[0] * (_rms(y) * g_ref[...])


def _proj_res(a, w, x, g, gate):
    b, s, d = x.shape
    kdim = a.shape[-1]
    tm = ROW_TILE
    return pl.pallas_call(
        _proj_res_kernel,
        grid=(b, s // tm),
        in_specs=[
            pl.BlockSpec((1, tm, kdim), lambda i, j: (i, j, 0)),
            pl.BlockSpec((kdim, d), lambda i, j: (0, 0)),
            pl.BlockSpec((1, tm, d), lambda i, j: (i, j, 0)),
            pl.BlockSpec((1, d), lambda i, j: (0, 0)),
            pl.BlockSpec((1, 1, d), lambda i, j: (i, 0, 0)),
        ],
        out_specs=pl.BlockSpec((1, tm, d), lambda i, j: (i, j, 0)),
        out_shape=jax.ShapeDtypeStruct((b, s, d), F32),
        compiler_params=_cparams("arbitrary", "arbitrary"),
        name="proj_norm_residual",
    )(a, w, x, g.reshape(1, d), gate.reshape(b, 1, d))


def _ffn_up_kernel(x_ref, g_ref, sh_ref, sc_ref, wg_ref, wu_ref, wc_ref, bc_ref, o_ref, gbuf_ref):
    tm = x_ref.shape[1]
    halo = SUBLANES

    @pl.when(pl.program_id(1) == 0)
    def _():
        gbuf_ref[0:halo, :] = jnp.zeros((halo, gbuf_ref.shape[1]), F32)

    x = x_ref[0]
    h = ((_rms(x) * g_ref[...]) * (1.0 + sc_ref[0]) + sh_ref[0]).astype(BF16)
    gate = _dot(h, wg_ref[...])
    gbuf_ref[halo:halo + tm, :] = gate
    g1 = gbuf_ref[halo - 1:halo - 1 + tm, :]
    g2 = gbuf_ref[halo - 2:halo - 2 + tm, :]
    cg = bc_ref[...] + wc_ref[0:1, :] * g2 + wc_ref[1:2, :] * g1 + wc_ref[2:3, :] * gate
    act = cg / (1.0 + jnp.exp(-cg))
    up = _dot(h, wu_ref[...])
    o_ref[0] = (act * up).astype(BF16)
    gbuf_ref[0:halo, :] = gbuf_ref[tm:tm + halo, :]


def _ffn_up(x, g, shift, scale, w_gate, w_up, w_conv, b_conv):
    b, s, d = x.shape
    f = w_gate.shape[1]
    tm = FFN_ROW_TILE
    cw = w_conv.shape[0]
    mod_spec = pl.BlockSpec((1, 1, d), lambda i, j: (i, 0, 0))
    return pl.pallas_call(
        _ffn_up_kernel,
        grid=(b, s // tm),
        in_specs=[
            pl.BlockSpec((1, tm, d), lambda i, j: (i, j, 0)),
            pl.BlockSpec((1, d), lambda i, j: (0, 0)),
            mod_spec, mod_spec,
            pl.BlockSpec((d, f), lambda i, j: (0, 0)),
            pl.BlockSpec((d, f), lambda i, j: (0, 0)),
            pl.BlockSpec((cw, f), lambda i, j: (0, 0)),
            pl.BlockSpec((1, f), lambda i, j: (0, 0)),
        ],
        out_specs=pl.BlockSpec((1, tm, f), lambda i, j: (i, j, 0)),
        out_shape=jax.ShapeDtypeStruct((b, s, f), BF16),
        scratch_shapes=[pltpu.VMEM((tm + SUBLANES, f), F32)],
        compiler_params=_cparams("arbitrary", "arbitrary"),
        name="ffn_gate_up_conv",
    )(x, g.reshape(1, d), shift.reshape(b, 1, d), scale.reshape(b, 1, d),
      w_gate, w_up, w_conv, b_conv.reshape(1, f))


def kernel(x, c, w_mod, b_mod, g_mix_pre, g_mix_post, w_qkv, w_o, w_fg, b_fg,
           g_ffn_pre, g_ffn_post, w_ffn_gate, w_ffn_up, w_conv, b_conv, w_ffn_down):
    depth = w_mod.shape[0]
    d = x.shape[-1]
    n_heads = w_fg.shape[-1]
    assert d == N_HEADS * HEAD_DIM and n_heads == N_HEADS and w_conv.shape[1] == 3
    scale = HEAD_DIM ** -0.5

    mod = _modulation(c, w_mod, b_mod)
    w_qkv_b = w_qkv.astype(BF16)
    w_o_b = w_o.astype(BF16)
    w_gate_b = w_ffn_gate.astype(BF16)
    w_up_b = w_ffn_up.astype(BF16)
    w_down_b = w_ffn_down.astype(BF16)
    w_fg_b = jnp.pad(w_fg, ((0, 0), (0, 0), (0, LANES - n_heads))).astype(BF16)
    b_fg_p = jnp.pad(b_fg, ((0, 0), (0, LANES - n_heads))).reshape(-1, 1, LANES)

    for i in range(depth):
        sh_a, sc_a, gt_a, sh_f, sc_f, gt_f = [mod[i, :, j * d:(j + 1) * d] for j in range(N_MOD)]
        if i % 2 == 0:
            q, k, v = _qkv(x, g_mix_pre[i], sh_a, sc_a, w_qkv_b[i], 0.5 * scale)
            o = _attention(q, k, v)
        else:
            j = i // 2
            q, k, v, cum = _qkv(x, g_mix_pre[i], sh_a, sc_a, w_qkv_b[i], scale,
                                w_fg_b[j], b_fg_p[j])
            o = _attention(q, k, v, cum)
        x = _proj_res(o, w_o_b[i], x, g_mix_post[i], gt_a)
        act = _ffn_up(x, g_ffn_pre[i], sh_f, sc_f, w_gate_b[i], w_up_b[i], w_conv[i], b_conv[i])
        x = _proj_res(act, w_down_b[i], x, g_ffn_post[i], gt_f)
    return x
```

```python
import functools

import jax
import jax.numpy as jnp
from jax import lax
from jax.experimental import pallas as pl
from jax.experimental.pallas import tpu as pltpu

F32 = jnp.float32
BF16 = jnp.bfloat16

N_HEADS = 16
HEAD_DIM = 64
N_MOD = 6
RMS_EPS = 1e-6
LANES = 128
SUBLANES = 8
HEADS_PER_LANE_TILE = LANES // HEAD_DIM
VMEM_LIMIT_BYTES = 56 * 1024 * 1024
NEG_BIG = -1e30

ATTN_TILE = 512
ATTN_LANE_TILES = 1
ROW_TILE = 1024
FFN_ROW_TILE = 512


def _cparams(*sem):
    return pltpu.CompilerParams(dimension_semantics=sem, vmem_limit_bytes=VMEM_LIMIT_BYTES)


def _split3(x):
    p1 = x.astype(BF16)
    r1 = x - p1.astype(F32)
    p2 = r1.astype(BF16)
    r2 = r1 - p2.astype(F32)
    return p1, p2, r2.astype(BF16)


def _dot(a, b):
    return jnp.dot(a, b, preferred_element_type=F32)


def _rms(x):
    return x * lax.rsqrt(jnp.mean(x * x, axis=-1, keepdims=True) + RMS_EPS)


def _mod_kernel(c_ref, w_ref, b_ref, o_ref):
    c = c_ref[...]
    ca = c / (1.0 + jnp.exp(-c))
    c1, c2, _ = _split3(ca)
    w = w_ref[0]
    w1 = w.astype(BF16)
    w2 = (w - w1.astype(F32)).astype(BF16)
    o_ref[0] = _dot(c1, w1) + _dot(c2, w1) + _dot(c1, w2) + b_ref[0]


def _modulation(c, w_mod, b_mod):
    depth, d, m = w_mod.shape
    b = c.shape[0]
    tn = 1536
    return pl.pallas_call(
        _mod_kernel,
        grid=(depth, m // tn),
        in_specs=[
            pl.BlockSpec((b, d), lambda i, j: (0, 0)),
            pl.BlockSpec((1, d, tn), lambda i, j: (i, 0, j)),
            pl.BlockSpec((1, 1, tn), lambda i, j: (i, 0, j)),
        ],
        out_specs=pl.BlockSpec((1, b, tn), lambda i, j: (i, 0, j)),
        out_shape=jax.ShapeDtypeStruct((depth, b, m), F32),
        compiler_params=_cparams("arbitrary", "arbitrary"),
        name="adaln_mod",
    )(c, w_mod, b_mod.reshape(depth, 1, m))


def _qkv_kernel(x_ref, g_ref, sh_ref, sc_ref, w_ref, *rest, d, q_scale, fox):
    if fox:
        wfg_ref, bfg_ref, q_ref, k_ref, v_ref, cum_ref, carry_ref = rest
    else:
        q_ref, k_ref, v_ref = rest
    x = x_ref[0]
    h = (_rms(x) * g_ref[...]) * (1.0 + sc_ref[0]) + sh_ref[0]
    hb = h.astype(BF16)
    q_ref[0] = (_dot(hb, w_ref[:, 0:d]) * q_scale).astype(BF16)
    k_ref[0] = _dot(hb, w_ref[:, d:2 * d]).astype(BF16)
    v_ref[0] = _dot(hb, w_ref[:, 2 * d:3 * d]).astype(BF16)
    if fox:
        tm = x.shape[0]

        @pl.when(pl.program_id(1) == 0)
        def _():
            carry_ref[...] = jnp.zeros_like(carry_ref)

        fl = _dot(hb, wfg_ref[...]) + bfg_ref[...]
        lf = jnp.minimum(fl, 0.0) - jnp.log(1.0 + jnp.exp(-jnp.abs(fl)))
        row = lax.broadcasted_iota(jnp.int32, (tm, tm), 0)
        col = lax.broadcasted_iota(jnp.int32, (tm, tm), 1)
        tri = jnp.where(row >= col, 1.0, 0.0).astype(BF16)
        p1, p2, p3 = _split3(lf)
        cs = _dot(tri, p1) + _dot(tri, p2) + _dot(tri, p3) + carry_ref[0:1, :]
        cum_ref[0] = cs
        carry_ref[...] = jnp.broadcast_to(cs[tm - 1:tm, :], carry_ref.shape)


def _qkv(x, g, shift, scale, w_qkv, q_scale, w_fg=None, b_fg=None):
    b, s, d = x.shape
    tm = ROW_TILE
    fox = w_fg is not None
    row_spec = pl.BlockSpec((1, tm, d), lambda i, j: (i, j, 0))
    vec_spec = pl.BlockSpec((1, d), lambda i, j: (0, 0))
    mod_spec = pl.BlockSpec((1, 1, d), lambda i, j: (i, 0, 0))
    in_specs = [row_spec, vec_spec, mod_spec, mod_spec,
                pl.BlockSpec((d, 3 * d), lambda i, j: (0, 0))]
    args = [x, g.reshape(1, d), shift.reshape(b, 1, d), scale.reshape(b, 1, d), w_qkv]
    out_specs = [row_spec, row_spec, row_spec]
    out_shape = [jax.ShapeDtypeStruct((b, s, d), BF16)] * 3
    scratch = []
    if fox:
        in_specs += [pl.BlockSpec((d, LANES), lambda i, j: (0, 0)),
                     pl.BlockSpec((1, LANES), lambda i, j: (0, 0))]
        args += [w_fg, b_fg]
        out_specs.append(pl.BlockSpec((1, tm, LANES), lambda i, j: (i, j, 0)))
        out_shape.append(jax.ShapeDtypeStruct((b, s, LANES), F32))
        scratch.append(pltpu.VMEM((SUBLANES, LANES), F32))
    return pl.pallas_call(
        functools.partial(_qkv_kernel, d=d, q_scale=q_scale, fox=fox),
        grid=(b, s // tm),
        in_specs=in_specs,
        out_specs=out_specs,
        out_shape=out_shape,
        scratch_shapes=scratch,
        compiler_params=_cparams("arbitrary", "arbitrary"),
        name="norm_qkv_fox" if fox else "norm_qkv",
    )(*args)


def _sublane_suffix_excl_prod(tot):
    n = SUBLANES
    rid = lax.broadcasted_iota(jnp.int32, tot.shape, 0)
    x = jnp.where(rid < n - 1, pltpu.roll(tot, n - 1, 0), 1.0)
    y = x * jnp.where(rid < n - 1, pltpu.roll(x, n - 1, 0), 1.0)
    y = y * jnp.where(rid < n - 2, pltpu.roll(y, n - 2, 0), 1.0)
    y = y * jnp.where(rid < n - 4, pltpu.roll(y, n - 4, 0), 1.0)
    return y


def _store_head_queries(qt_ref, g0, i, q_tile):
    qt = q_tile.astype(F32).T
    row = lax.broadcasted_iota(jnp.int32, qt.shape, 0)
    for h in range(HEADS_PER_LANE_TILE):
        mine = (row >= h * HEAD_DIM) & (row < (h + 1) * HEAD_DIM)
        qt_ref[g0 + h, i] = jnp.where(mine, qt, 0.0).astype(BF16)


def _lane_tile(p):
    return pl.ds(p * LANES, LANES)


def _head_rows(g):
    return pl.ds((g % HEADS_PER_LANE_TILE) * HEAD_DIM, HEAD_DIM)


def _advance(qi, kj):
    last = kj == 0
    return jnp.where(last, qi + 1, qi), jnp.where(last, qi + 1, kj - 1)


def _walk_tiles(nt, first_step, later_step, finish, state0):
    n_steps = nt * (nt + 1) // 2
    zero = jnp.int32(0)

    def clamp(p):
        return jnp.minimum(p[0], nt - 1), jnp.minimum(p[1], nt - 1)

    cur = (zero, zero)
    nxt = _advance(*cur)
    state = first_step(None, cur, clamp(nxt), state0)

    def body(_, c):
        prev, cur, state = (c[0], c[1]), (c[2], c[3]), c[4]
        nxt = _advance(*cur)
        state = lax.cond(cur[1] == cur[0],
                         lambda st: first_step(prev, cur, clamp(nxt), st),
                         lambda st: later_step(prev, cur, clamp(nxt), st), state)
        return (*cur, *nxt, state)

    c = lax.fori_loop(1, n_steps, body, (*cur, *nxt, state))
    finish((c[0], c[1]), c[4])


def _sb_kernel(q_ref, k_ref, v_ref, o_ref, kp_ref, vt_ref, tmp_ref, qt_ref, z_ref, w_ref, a_ref,
               acc_ref, *, t):
    s = q_ref.shape[1]
    nt = s // t
    m_rows = t // SUBLANES
    lane_tiles = range(q_ref.shape[2] // LANES)
    heads = range(len(lane_tiles) * HEADS_PER_LANE_TILE)

    def prep(i, carry):
        base = pl.multiple_of(i * t, t)
        for p in lane_tiles:
            for r in range(SUBLANES):
                tmp_ref[pl.ds(r, m_rows, stride=SUBLANES), :] = (
                    k_ref[0, pl.ds(base + r * m_rows, m_rows), _lane_tile(p)].astype(F32))
            kp_ref[p, i] = tmp_ref[...].astype(BF16)
            for r in range(SUBLANES):
                tmp_ref[pl.ds(r, m_rows, stride=SUBLANES), :] = (
                    v_ref[0, pl.ds(base + r * m_rows, m_rows), _lane_tile(p)].astype(F32))
            vt_ref[p, i] = tmp_ref[...].T.astype(BF16)
            _store_head_queries(qt_ref, p * HEADS_PER_LANE_TILE, i,
                                q_ref[0, pl.ds(base, t), _lane_tile(p)])
        return carry

    lax.fori_loop(0, nt, prep, 0)

    rho = lax.broadcasted_iota(jnp.int32, (t, t), 0)
    key_in_tile = (rho & (SUBLANES - 1)) * m_rows + (rho >> 3)
    qry_in_tile = lax.broadcasted_iota(jnp.int32, (t, t), 1)
    strict = key_in_tile < qry_in_tile
    ones = jnp.ones((SUBLANES, t), F32)
    reread = jnp.minimum(pl.program_id(0), 0)

    def scores(pos):
        return [_dot(kp_ref[g // HEADS_PER_LANE_TILE, pos[1]], qt_ref[g, pos[0]]) for g in heads]

    def values(pos):
        return jnp.concatenate(
            [_dot(vt_ref[g // HEADS_PER_LANE_TILE, pos[1], _head_rows(g), :], a_ref[g])
             for g in heads], axis=0)

    def weights(g, carry_p, mask):
        omb = 0.5 - 0.5 * jnp.tanh(z_ref[g])
        if mask is not None:
            omb = jnp.where(mask, omb, 1.0)
        w_ref[g] = omb
        parts = [omb[m * SUBLANES:(m + 1) * SUBLANES, :] for m in range(m_rows)]
        while len(parts) > 1:
            parts = [parts[i] * parts[i + 1] for i in range(0, len(parts), 2)]
        run = _sublane_suffix_excl_prod(parts[0]) * carry_p
        pack = 2 * SUBLANES
        for j in reversed(range(t // pack)):
            w = w_ref[g + reread, pl.ds(j * pack, pack), :]
            mid = run * w[SUBLANES:, :]
            low = mid * w[:SUBLANES, :]
            a_ref[g, pl.ds(j * pack, pack), :] = jnp.concatenate(
                [mid - low, run - mid], axis=0).astype(BF16)
            run = low
        return jnp.broadcast_to(run[0:1, :], (SUBLANES, t))

    def write_out(pos, out):
        rows = pl.ds(pl.multiple_of(pos[0] * t, t), t)
        for p in lane_tiles:
            o_ref[0, rows, _lane_tile(p)] = out[p * LANES:(p + 1) * LANES, :].T.astype(BF16)

    def first_step(prev, cur, nxt, state):
        if prev is None:
            zs = scores(cur)
            for g in heads:
                z_ref[g] = zs[g]
        zs = scores(nxt)
        if prev is not None:
            write_out(prev, acc_ref[...] + values(prev))
        acc_ref[...] = jnp.zeros_like(acc_ref)
        state = tuple(weights(g, ones, strict) for g in heads)
        for g in heads:
            z_ref[g] = zs[g]
        return state

    def later_step(prev, cur, nxt, state):
        zs = scores(nxt)
        acc_ref[...] += values(prev)
        state = tuple(weights(g, state[g], None) for g in heads)
        for g in heads:
            z_ref[g] = zs[g]
        return state

    def finish(prev, state):
        write_out(prev, acc_ref[...] + values(prev))

    _walk_tiles(nt, first_step, later_step, finish, tuple(ones for _ in heads))


def _fox_kernel(q_ref, k_ref, v_ref, cum_ref, o_ref, vt_ref, bias_ref, qt_ref, z_ref, w_ref, p_ref,
                acc_ref, *, t):
    s = q_ref.shape[1]
    nt = s // t
    lane_tiles = range(q_ref.shape[2] // LANES)
    heads = range(len(lane_tiles) * HEADS_PER_LANE_TILE)
    head0 = pl.program_id(1) * len(heads)

    def prep(i, carry):
        base = pl.multiple_of(i * t, t)
        for p in lane_tiles:
            vt_ref[p, i] = v_ref[0, pl.ds(base, t), _lane_tile(p)].astype(F32).T.astype(BF16)
            _store_head_queries(qt_ref, p * HEADS_PER_LANE_TILE, i,
                                q_ref[0, pl.ds(base, t), _lane_tile(p)])
        cum = cum_ref[0, pl.ds(base, t), :]
        lane = lax.broadcasted_iota(jnp.int32, cum.shape, 1)
        for g in heads:
            col = jnp.sum(jnp.where(lane == head0 + g, cum, 0.0), axis=1, keepdims=True)
            bias_ref[g, i] = jnp.broadcast_to(col, (t, LANES))
        return carry

    lax.fori_loop(0, nt, prep, 0)

    key_in_tile = lax.broadcasted_iota(jnp.int32, (t, t), 0)
    qry_in_tile = lax.broadcasted_iota(jnp.int32, (t, t), 1)
    causal = key_in_tile <= qry_in_tile
    rep = t // LANES
    fresh = (jnp.full((1, t), NEG_BIG, F32), jnp.zeros((1, t), F32), jnp.zeros((1, t), F32))
    reread = jnp.minimum(pl.program_id(0), 0)

    def scores(pos):
        rows = pl.ds(pl.multiple_of(pos[1] * t, t), t)
        return [_dot(k_ref[0, rows, _lane_tile(g // HEADS_PER_LANE_TILE)], qt_ref[g, pos[0]])
                for g in heads]

    def values(pos):
        return [_dot(vt_ref[g // HEADS_PER_LANE_TILE, pos[1], _head_rows(g), :], p_ref[g])
                for g in heads]

    def weights(g, pos, st, mask):
        m_run, l_run, _ = st
        z = z_ref[g] - jnp.concatenate([bias_ref[g, pos[1]]] * rep, axis=1)
        if mask is not None:
            z = jnp.where(mask, z, NEG_BIG)
        w_ref[g] = z
        m_new = jnp.maximum(m_run, jnp.max(z, axis=0, keepdims=True))
        p = jnp.exp(w_ref[g + reread] - m_new)
        alpha = jnp.exp(m_run - m_new)
        p_ref[g] = p.astype(BF16)
        return m_new, alpha * l_run + jnp.sum(p, axis=0, keepdims=True), alpha

    def rescaled(g, av, st):
        return acc_ref[pl.ds(g * HEAD_DIM, HEAD_DIM), :] * st[2] + av[g]

    def write_out(pos, av, state):
        rows = pl.ds(pl.multiple_of(pos[0] * t, t), t)
        for p in lane_tiles:
            out = jnp.concatenate(
                [rescaled(g, av, state[g]) / state[g][1]
                 for g in range(p * HEADS_PER_LANE_TILE, (p + 1) * HEADS_PER_LANE_TILE)], axis=0)
            o_ref[0, rows, _lane_tile(p)] = out.T.astype(BF16)

    def first_step(prev, cur, nxt, state):
        if prev is None:
            zs = scores(cur)
            for g in heads:
                z_ref[g] = zs[g]
        zs = scores(nxt)
        if prev is not None:
            write_out(prev, values(prev), state)
        acc_ref[...] = jnp.zeros_like(acc_ref)
        state = tuple(weights(g, cur, fresh, causal) for g in heads)
        for g in heads:
            z_ref[g] = zs[g]
        return state

    def later_step(prev, cur, nxt, state):
        zs = scores(nxt)
        av = values(prev)
        for g in heads:
            acc_ref[pl.ds(g * HEAD_DIM, HEAD_DIM), :] = rescaled(g, av, state[g])
        state = tuple(weights(g, cur, state[g], None) for g in heads)
        for g in heads:
            z_ref[g] = zs[g]
        return state

    def finish(prev, state):
        write_out(prev, values(prev), state)

    _walk_tiles(nt, first_step, later_step, finish, tuple(fresh for _ in heads))


def _attention(q, k, v, cum=None):
    b, s, d = q.shape
    t = ATTN_TILE
    nt = s // t
    n_lt = ATTN_LANE_TILES
    n_heads = n_lt * HEADS_PER_LANE_TILE
    width = n_lt * LANES
    blk = pl.BlockSpec((1, s, width), lambda i, j: (i, 0, j))
    in_specs = [blk, blk, blk]
    args = [q, k, v]
    pipe = [pltpu.VMEM((n_heads, nt, LANES, t), BF16),
            pltpu.VMEM((n_heads, t, t), F32), pltpu.VMEM((n_heads, t, t), F32),
            pltpu.VMEM((n_heads, t, t), BF16), pltpu.VMEM((width, t), F32)]
    if cum is None:
        body = functools.partial(_sb_kernel, t=t)
        scratch = [pltpu.VMEM((n_lt, nt, t, LANES), BF16), pltpu.VMEM((n_lt, nt, LANES, t), BF16),
                   pltpu.VMEM((t, LANES), F32)] + pipe
        name = "stickbreak_attn"
    else:
        body = functools.partial(_fox_kernel, t=t)
        in_specs.append(pl.BlockSpec((1, s, LANES), lambda i, j: (i, 0, 0)))
        args.append(cum)
        scratch = [pltpu.VMEM((n_lt, nt, LANES, t), BF16),
                   pltpu.VMEM((n_heads, nt, t, LANES), F32)] + pipe
        name = "forgetting_attn"
    return pl.pallas_call(
        body,
        grid=(b, d // width),
        in_specs=in_specs,
        out_specs=blk,
        out_shape=jax.ShapeDtypeStruct((b, s, d), BF16),
        scratch_shapes=scratch,
        compiler_params=_cparams("arbitrary", "arbitrary"),
        name=name,
    )(*args)


def _proj_res_kernel(a_ref, w_ref, x_ref, g_ref, gt_ref, o_ref):
    y = _dot(a_ref[0], w_ref[...])
    o_ref[0] = x_ref[0] + gt_ref[0] * (_rms(y) * g_ref[...])


def _proj_res(a, w, x, g, gate):
    b, s, d = x.shape
    kdim = a.shape[-1]
    tm = ROW_TILE
    return pl.pallas_call(
        _proj_res_kernel,
        grid=(b, s // tm),
        in_specs=[
            pl.BlockSpec((1, tm, kdim), lambda i, j: (i, j, 0)),
            pl.BlockSpec((kdim, d), lambda i, j: (0, 0)),
            pl.BlockSpec((1, tm, d), lambda i, j: (i, j, 0)),
            pl.BlockSpec((1, d), lambda i, j: (0, 0)),
            pl.BlockSpec((1, 1, d), lambda i, j: (i, 0, 0)),
        ],
        out_specs=pl.BlockSpec((1, tm, d), lambda i, j: (i, j, 0)),
        out_shape=jax.ShapeDtypeStruct((b, s, d), F32),
        compiler_params=_cparams("arbitrary", "arbitrary"),
        name="proj_norm_residual",
    )(a, w, x, g.reshape(1, d), gate.reshape(b, 1, d))


def _ffn_up_kernel(x_ref, g_ref, sh_ref, sc_ref, wg_ref, wu_ref, wc_ref, bc_ref, o_ref, gbuf_ref):
    tm = x_ref.shape[1]
    halo = SUBLANES

    @pl.when(pl.program_id(1) == 0)
    def _():
        gbuf_ref[0:halo, :] = jnp.zeros((halo, gbuf_ref.shape[1]), F32)

    x = x_ref[0]
    h = ((_rms(x) * g_ref[...]) * (1.0 + sc_ref[0]) + sh_ref[0]).astype(BF16)
    gate = _dot(h, wg_ref[...])
    gbuf_ref[halo:halo + tm, :] = gate
    g1 = gbuf_ref[halo - 1:halo - 1 + tm, :]
    g2 = gbuf_ref[halo - 2:halo - 2 + tm, :]
    cg = bc_ref[...] + wc_ref[0:1, :] * g2 + wc_ref[1:2, :] * g1 + wc_ref[2:3, :] * gate
    act = cg / (1.0 + jnp.exp(-cg))
    up = _dot(h, wu_ref[...])
    o_ref[0] = (act * up).astype(BF16)
    gbuf_ref[0:halo, :] = gbuf_ref[tm:tm + halo, :]


def _ffn_up(x, g, shift, scale, w_gate, w_up, w_conv, b_conv):
    b, s, d = x.shape
    f = w_gate.shape[1]
    tm = FFN_ROW_TILE
    cw = w_conv.shape[0]
    mod_spec = pl.BlockSpec((1, 1, d), lambda i, j: (i, 0, 0))
    return pl.pallas_call(
        _ffn_up_kernel,
        grid=(b, s // tm),
        in_specs=[
            pl.BlockSpec((1, tm, d), lambda i, j: (i, j, 0)),
            pl.BlockSpec((1, d), lambda i, j: (0, 0)),
            mod_spec, mod_spec,
            pl.BlockSpec((d, f), lambda i, j: (0, 0)),
            pl.BlockSpec((d, f), lambda i, j: (0, 0)),
            pl.BlockSpec((cw, f), lambda i, j: (0, 0)),
            pl.BlockSpec((1, f), lambda i, j: (0, 0)),
        ],
        out_specs=pl.BlockSpec((1, tm, f), lambda i, j: (i, j, 0)),
        out_shape=jax.ShapeDtypeStruct((b, s, f), BF16),
        scratch_shapes=[pltpu.VMEM((tm + SUBLANES, f), F32)],
        compiler_params=_cparams("arbitrary", "arbitrary"),
        name="ffn_gate_up_conv",
    )(x, g.reshape(1, d), shift.reshape(b, 1, d), scale.reshape(b, 1, d),
      w_gate, w_up, w_conv, b_conv.reshape(1, f))


def kernel(x, c, w_mod, b_mod, g_mix_pre, g_mix_post, w_qkv, w_o, w_fg, b_fg,
           g_ffn_pre, g_ffn_post, w_ffn_gate, w_ffn_up, w_conv, b_conv, w_ffn_down):
    depth = w_mod.shape[0]
    d = x.shape[-1]
    n_heads = w_fg.shape[-1]
    assert d == N_HEADS * HEAD_DIM and n_heads == N_HEADS and w_conv.shape[1] == 3
    scale = HEAD_DIM ** -0.5

    mod = _modulation(c, w_mod, b_mod)
    w_qkv_b = w_qkv.astype(BF16)
    w_o_b = w_o.astype(BF16)
    w_gate_b = w_ffn_gate.astype(BF16)
    w_up_b = w_ffn_up.astype(BF16)
    w_down_b = w_ffn_down.astype(BF16)
    w_fg_b = jnp.pad(w_fg, ((0, 0), (0, 0), (0, LANES - n_heads))).astype(BF16)
    b_fg_p = jnp.pad(b_fg, ((0, 0), (0, LANES - n_heads))).reshape(-1, 1, LANES)

    for i in range(depth):
        sh_a, sc_a, gt_a, sh_f, sc_f, gt_f = [mod[i, :, j * d:(j + 1) * d] for j in range(N_MOD)]
        if i % 2 == 0:
            q, k, v = _qkv(x, g_mix_pre[i], sh_a, sc_a, w_qkv_b[i], 0.5 * scale)
            o = _attention(q, k, v)
        else:
            j = i // 2
            q, k, v, cum = _qkv(x, g_mix_pre[i], sh_a, sc_a, w_qkv_b[i], scale,
                                w_fg_b[j], b_fg_p[j])
            o = _attention(q, k, v, cum)
        x = _proj_res(o, w_o_b[i], x, g_mix_post[i], gt_a)
        act = _ffn_up(x, g_ffn_pre[i], sh_f, sc_f, w_gate_b[i], w_up_b[i], w_conv[i], b_conv[i])
        x = _proj_res(act, w_down_b[i], x, g_ffn_post[i], gt_f)
    return x
```

```python
import functools

import jax
import jax.numpy as jnp
from jax import lax
from jax.experimental import pallas as pl
from jax.experimental.pallas import tpu as pltpu

F32 = jnp.float32
BF16 = jnp.bfloat16

N_HEADS = 16
HEAD_DIM = 64
N_MOD = 6
RMS_EPS = 1e-6
LANES = 128
SUBLANES = 8
HEADS_PER_LANE_TILE = LANES // HEAD_DIM
VMEM_LIMIT_BYTES = 56 * 1024 * 1024
NEG_BIG = -1e30
LOG2_E = 1.4426950408889634

ATTN_TILE = 512
ATTN_LANE_TILES = 1
ROW_TILE = 1024
FFN_ROW_TILE = 512
CUMSUM_CHUNK = 256


def _cparams(*sem):
    return pltpu.CompilerParams(dimension_semantics=sem, vmem_limit_bytes=VMEM_LIMIT_BYTES)


def _split3(x):
    p1 = x.astype(BF16)
    r1 = x - p1.astype(F32)
    p2 = r1.astype(BF16)
    r2 = r1 - p2.astype(F32)
    return p1, p2, r2.astype(BF16)


def _dot(a, b):
    return jnp.dot(a, b, preferred_element_type=F32)


def _rms(x):
    return x * lax.rsqrt(jnp.mean(x * x, axis=-1, keepdims=True) + RMS_EPS)


def _mod_kernel(c_ref, w_ref, b_ref, o_ref):
    c = c_ref[...]
    ca = c / (1.0 + jnp.exp(-c))
    c1, c2, _ = _split3(ca)
    w = w_ref[0]
    w1 = w.astype(BF16)
    w2 = (w - w1.astype(F32)).astype(BF16)
    o_ref[0] = _dot(c1, w1) + _dot(c2, w1) + _dot(c1, w2) + b_ref[0]


def _modulation(c, w_mod, b_mod):
    depth, d, m = w_mod.shape
    b = c.shape[0]
    tn = 1536
    return pl.pallas_call(
        _mod_kernel,
        grid=(depth, m // tn),
        in_specs=[
            pl.BlockSpec((b, d), lambda i, j: (0, 0)),
            pl.BlockSpec((1, d, tn), lambda i, j: (i, 0, j)),
            pl.BlockSpec((1, 1, tn), lambda i, j: (i, 0, j)),
        ],
        out_specs=pl.BlockSpec((1, b, tn), lambda i, j: (i, 0, j)),
        out_shape=jax.ShapeDtypeStruct((depth, b, m), F32),
        compiler_params=_cparams("arbitrary", "arbitrary"),
        name="adaln_mod",
    )(c, w_mod, b_mod.reshape(depth, 1, m))


def _qkv_kernel(x_ref, g_ref, sh_ref, sc_ref, w_ref, *rest, d, q_scale, fox):
    if fox:
        wfg_ref, bfg_ref, q_ref, k_ref, v_ref, cum_ref, carry_ref = rest
    else:
        q_ref, k_ref, v_ref = rest
    x = x_ref[0]
    h = (_rms(x) * g_ref[...]) * (1.0 + sc_ref[0]) + sh_ref[0]
    hb = h.astype(BF16)
    q_ref[0] = (_dot(hb, w_ref[:, 0:d]) * q_scale).astype(BF16)
    k_ref[0] = _dot(hb, w_ref[:, d:2 * d]).astype(BF16)
    v_ref[0] = _dot(hb, w_ref[:, 2 * d:3 * d]).astype(BF16)
    if fox:
        tm = x.shape[0]

        @pl.when(pl.program_id(1) == 0)
        def _():
            carry_ref[...] = jnp.zeros_like(carry_ref)

        fl = _dot(hb, wfg_ref[...]) + bfg_ref[...]
        lf = jnp.minimum(fl, 0.0) - jnp.log(1.0 + jnp.exp(-jnp.abs(fl)))
        ch = CUMSUM_CHUNK
        row = lax.broadcasted_iota(jnp.int32, (ch, ch), 0)
        col = lax.broadcasted_iota(jnp.int32, (ch, ch), 1)
        tri = jnp.where(row >= col, 1.0, 0.0).astype(BF16)
        offset = carry_ref[0:1, :]
        for c0 in range(0, tm, ch):
            p1, p2, p3 = _split3(lf[c0:c0 + ch, :])
            cs = _dot(tri, p1) + _dot(tri, p2) + _dot(tri, p3) + offset
            cum_ref[0, c0:c0 + ch, :] = cs
            offset = cs[ch - 1:ch, :]
        carry_ref[...] = jnp.broadcast_to(offset, carry_ref.shape)


def _qkv(x, g, shift, scale, w_qkv, q_scale, w_fg=None, b_fg=None):
    b, s, d = x.shape
    tm = ROW_TILE
    fox = w_fg is not None
    row_spec = pl.BlockSpec((1, tm, d), lambda i, j: (i, j, 0))
    vec_spec = pl.BlockSpec((1, d), lambda i, j: (0, 0))
    mod_spec = pl.BlockSpec((1, 1, d), lambda i, j: (i, 0, 0))
    in_specs = [row_spec, vec_spec, mod_spec, mod_spec,
                pl.BlockSpec((d, 3 * d), lambda i, j: (0, 0))]
    args = [x, g.reshape(1, d), shift.reshape(b, 1, d), scale.reshape(b, 1, d), w_qkv]
    out_specs = [row_spec, row_spec, row_spec]
    out_shape = [jax.ShapeDtypeStruct((b, s, d), BF16)] * 3
    scratch = []
    if fox:
        in_specs += [pl.BlockSpec((d, LANES), lambda i, j: (0, 0)),
                     pl.BlockSpec((1, LANES), lambda i, j: (0, 0))]
        args += [w_fg, b_fg]
        out_specs.append(pl.BlockSpec((1, tm, LANES), lambda i, j: (i, j, 0)))
        out_shape.append(jax.ShapeDtypeStruct((b, s, LANES), F32))
        scratch.append(pltpu.VMEM((SUBLANES, LANES), F32))
    return pl.pallas_call(
        functools.partial(_qkv_kernel, d=d, q_scale=q_scale, fox=fox),
        grid=(b, s // tm),
        in_specs=in_specs,
        out_specs=out_specs,
        out_shape=out_shape,
        scratch_shapes=scratch,
        compiler_params=_cparams("arbitrary", "arbitrary"),
        name="norm_qkv_fox" if fox else "norm_qkv",
    )(*args)


def _sublane_suffix_excl_prod(tot):
    n = SUBLANES
    rid = lax.broadcasted_iota(jnp.int32, tot.shape, 0)
    x = jnp.where(rid < n - 1, pltpu.roll(tot, n - 1, 0), 1.0)
    y = x * jnp.where(rid < n - 1, pltpu.roll(x, n - 1, 0), 1.0)
    y = y * jnp.where(rid < n - 2, pltpu.roll(y, n - 2, 0), 1.0)
    y = y * jnp.where(rid < n - 4, pltpu.roll(y, n - 4, 0), 1.0)
    return y


def _store_head_queries(qt_ref, g0, i, q_tile):
    qt = q_tile.astype(F32).T
    row = lax.broadcasted_iota(jnp.int32, qt.shape, 0)
    for h in range(HEADS_PER_LANE_TILE):
        mine = (row >= h * HEAD_DIM) & (row < (h + 1) * HEAD_DIM)
        qt_ref[g0 + h, i] = jnp.where(mine, qt, 0.0).astype(BF16)


def _lane_tile(p):
    return pl.ds(p * LANES, LANES)


def _head_rows(g):
    return pl.ds((g % HEADS_PER_LANE_TILE) * HEAD_DIM, HEAD_DIM)


def _advance(qi, kj):
    last = kj == 0
    return jnp.where(last, qi + 1, qi), jnp.where(last, qi + 1, kj - 1)


def _walk_tiles(nt, first_step, later_step, finish, state0):
    n_steps = nt * (nt + 1) // 2
    zero = jnp.int32(0)

    def clamp(p):
        return jnp.minimum(p[0], nt - 1), jnp.minimum(p[1], nt - 1)

    cur = (zero, zero)
    nxt = _advance(*cur)
    state = first_step(None, cur, clamp(nxt), state0)

    def body(_, c):
        prev, cur, state = (c[0], c[1]), (c[2], c[3]), c[4]
        nxt = _advance(*cur)
        state = lax.cond(cur[1] == cur[0],
                         lambda st: first_step(prev, cur, clamp(nxt), st),
                         lambda st: later_step(prev, cur, clamp(nxt), st), state)
        return (*cur, *nxt, state)

    c = lax.fori_loop(1, n_steps, body, (*cur, *nxt, state))
    finish((c[0], c[1]), c[4])


def _sb_kernel(q_ref, k_ref, v_ref, o_ref, kp_ref, vt_ref, tmp_ref, qt_ref, z_ref, w_ref, a_ref,
               acc_ref, *, t):
    s = q_ref.shape[1]
    nt = s // t
    m_rows = t // SUBLANES
    lane_tiles = range(q_ref.shape[2] // LANES)
    heads = range(len(lane_tiles) * HEADS_PER_LANE_TILE)

    def prep(i, carry):
        base = pl.multiple_of(i * t, t)
        for p in lane_tiles:
            for r in range(SUBLANES):
                tmp_ref[pl.ds(r, m_rows, stride=SUBLANES), :] = (
                    k_ref[0, pl.ds(base + r * m_rows, m_rows), _lane_tile(p)].astype(F32))
            kp_ref[p, i] = tmp_ref[...].astype(BF16)
            for r in range(SUBLANES):
                tmp_ref[pl.ds(r, m_rows, stride=SUBLANES), :] = (
                    v_ref[0, pl.ds(base + r * m_rows, m_rows), _lane_tile(p)].astype(F32))
            vt_ref[p, i] = tmp_ref[...].T.astype(BF16)
            _store_head_queries(qt_ref, p * HEADS_PER_LANE_TILE, i,
                                q_ref[0, pl.ds(base, t), _lane_tile(p)])
        return carry

    lax.fori_loop(0, nt, prep, 0)

    rho = lax.broadcasted_iota(jnp.int32, (t, t), 0)
    key_in_tile = (rho & (SUBLANES - 1)) * m_rows + (rho >> 3)
    qry_in_tile = lax.broadcasted_iota(jnp.int32, (t, t), 1)
    strict = key_in_tile < qry_in_tile
    ones = jnp.ones((SUBLANES, t), F32)
    reread = jnp.minimum(pl.program_id(0), 0)

    def scores(pos):
        return [_dot(kp_ref[g // HEADS_PER_LANE_TILE, pos[1]], qt_ref[g, pos[0]]) for g in heads]

    def values(pos):
        return jnp.concatenate(
            [_dot(vt_ref[g // HEADS_PER_LANE_TILE, pos[1], _head_rows(g), :], a_ref[g])
             for g in heads], axis=0)

    def weights(g, carry_p, mask):
        omb = 0.5 - 0.5 * jnp.tanh(z_ref[g])
        if mask is not None:
            omb = jnp.where(mask, omb, 1.0)
        w_ref[g] = omb
        parts = [omb[m * SUBLANES:(m + 1) * SUBLANES, :] for m in range(m_rows)]
        while len(parts) > 1:
            parts = [parts[i] * parts[i + 1] for i in range(0, len(parts), 2)]
        run = _sublane_suffix_excl_prod(parts[0]) * carry_p
        pack = 2 * SUBLANES
        for j in reversed(range(t // pack)):
            w = w_ref[g + reread, pl.ds(j * pack, pack), :]
            mid = run * w[SUBLANES:, :]
            low = mid * w[:SUBLANES, :]
            a_ref[g, pl.ds(j * pack, pack), :] = jnp.concatenate(
                [mid - low, run - mid], axis=0).astype(BF16)
            run = low
        return jnp.broadcast_to(run[0:1, :], (SUBLANES, t))

    def write_out(pos, out):
        rows = pl.ds(pl.multiple_of(pos[0] * t, t), t)
        for p in lane_tiles:
            o_ref[0, rows, _lane_tile(p)] = out[p * LANES:(p + 1) * LANES, :].T.astype(BF16)

    def first_step(prev, cur, nxt, state):
        if prev is None:
            zs = scores(cur)
            for g in heads:
                z_ref[g] = zs[g]
        zs = scores(nxt)
        if prev is not None:
            write_out(prev, acc_ref[...] + values(prev))
        acc_ref[...] = jnp.zeros_like(acc_ref)
        state = tuple(weights(g, ones, strict) for g in heads)
        for g in heads:
            z_ref[g] = zs[g]
        return state

    def later_step(prev, cur, nxt, state):
        zs = scores(nxt)
        acc_ref[...] += values(prev)
        state = tuple(weights(g, state[g], None) for g in heads)
        for g in heads:
            z_ref[g] = zs[g]
        return state

    def finish(prev, state):
        write_out(prev, acc_ref[...] + values(prev))

    _walk_tiles(nt, first_step, later_step, finish, tuple(ones for _ in heads))


def _fox_kernel(q_ref, k_ref, v_ref, cum_ref, o_ref, vt_ref, bias_ref, qt_ref, z_ref, w_ref, p_ref,
                acc_ref, *, t):
    s = q_ref.shape[1]
    nt = s // t
    lane_tiles = range(q_ref.shape[2] // LANES)
    heads = range(len(lane_tiles) * HEADS_PER_LANE_TILE)
    head0 = pl.program_id(1) * len(heads)

    def prep(i, carry):
        base = pl.multiple_of(i * t, t)
        for p in lane_tiles:
            vt_ref[p, i] = v_ref[0, pl.ds(base, t), _lane_tile(p)].astype(F32).T.astype(BF16)
            _store_head_queries(qt_ref, p * HEADS_PER_LANE_TILE, i,
                                q_ref[0, pl.ds(base, t), _lane_tile(p)])
        cum = cum_ref[0, pl.ds(base, t), :]
        lane = lax.broadcasted_iota(jnp.int32, cum.shape, 1)
        for g in heads:
            col = jnp.sum(jnp.where(lane == head0 + g, cum, 0.0), axis=1, keepdims=True)
            bias_ref[g, i] = jnp.broadcast_to(col * LOG2_E, (t, LANES))
        return carry

    lax.fori_loop(0, nt, prep, 0)

    key_in_tile = lax.broadcasted_iota(jnp.int32, (t, t), 0)
    qry_in_tile = lax.broadcasted_iota(jnp.int32, (t, t), 1)
    causal = key_in_tile <= qry_in_tile
    rep = t // LANES
    fresh = (jnp.full((1, t), NEG_BIG, F32), jnp.zeros((1, t), F32), jnp.zeros((1, t), F32))
    reread = jnp.minimum(pl.program_id(0), 0)

    def scores(pos):
        rows = pl.ds(pl.multiple_of(pos[1] * t, t), t)
        return [_dot(k_ref[0, rows, _lane_tile(g // HEADS_PER_LANE_TILE)], qt_ref[g, pos[0]])
                for g in heads]

    def values(pos):
        return [_dot(vt_ref[g // HEADS_PER_LANE_TILE, pos[1], _head_rows(g), :], p_ref[g])
                for g in heads]

    def weights(g, pos, st, mask):
        m_run, l_run, _ = st
        z = z_ref[g] - jnp.concatenate([bias_ref[g, pos[1]]] * rep, axis=1)
        if mask is not None:
            z = jnp.where(mask, z, NEG_BIG)
        w_ref[g] = z
        m_new = jnp.maximum(m_run, jnp.max(z, axis=0, keepdims=True))
        p = jnp.exp2(w_ref[g + reread] - m_new)
        alpha = jnp.exp2(m_run - m_new)
        p_ref[g] = p.astype(BF16)
        return m_new, alpha * l_run + jnp.sum(p, axis=0, keepdims=True), alpha

    def rescaled(g, av, st):
        return acc_ref[pl.ds(g * HEAD_DIM, HEAD_DIM), :] * st[2] + av[g]

    def write_out(pos, av, state):
        rows = pl.ds(pl.multiple_of(pos[0] * t, t), t)
        for p in lane_tiles:
            out = jnp.concatenate(
                [rescaled(g, av, state[g]) / state[g][1]
                 for g in range(p * HEADS_PER_LANE_TILE, (p + 1) * HEADS_PER_LANE_TILE)], axis=0)
            o_ref[0, rows, _lane_tile(p)] = out.T.astype(BF16)

    def first_step(prev, cur, nxt, state):
        if prev is None:
            zs = scores(cur)
            for g in heads:
                z_ref[g] = zs[g]
        zs = scores(nxt)
        if prev is not None:
            write_out(prev, values(prev), state)
        acc_ref[...] = jnp.zeros_like(acc_ref)
        state = tuple(weights(g, cur, fresh, causal) for g in heads)
        for g in heads:
            z_ref[g] = zs[g]
        return state

    def later_step(prev, cur, nxt, state):
        zs = scores(nxt)
        av = values(prev)
        for g in heads:
            acc_ref[pl.ds(g * HEAD_DIM, HEAD_DIM), :] = rescaled(g, av, state[g])
        state = tuple(weights(g, cur, state[g], None) for g in heads)
        for g in heads:
            z_ref[g] = zs[g]
        return state

    def finish(prev, state):
        write_out(prev, values(prev), state)

    _walk_tiles(nt, first_step, later_step, finish, tuple(fresh for _ in heads))


def _attention(q, k, v, cum=None):
    b, s, d = q.shape
    t = ATTN_TILE
    nt = s // t
    n_lt = ATTN_LANE_TILES
    n_heads = n_lt * HEADS_PER_LANE_TILE
    width = n_lt * LANES
    blk = pl.BlockSpec((1, s, width), lambda i, j: (i, 0, j))
    in_specs = [blk, blk, blk]
    args = [q, k, v]
    pipe = [pltpu.VMEM((n_heads, nt, LANES, t), BF16),
            pltpu.VMEM((n_heads, t, t), F32), pltpu.VMEM((n_heads, t, t), F32),
            pltpu.VMEM((n_heads, t, t), BF16), pltpu.VMEM((width, t), F32)]
    if cum is None:
        body = functools.partial(_sb_kernel, t=t)
        scratch = [pltpu.VMEM((n_lt, nt, t, LANES), BF16), pltpu.VMEM((n_lt, nt, LANES, t), BF16),
                   pltpu.VMEM((t, LANES), F32)] + pipe
        name = "stickbreak_attn"
    else:
        body = functools.partial(_fox_kernel, t=t)
        in_specs.append(pl.BlockSpec((1, s, LANES), lambda i, j: (i, 0, 0)))
        args.append(cum)
        scratch = [pltpu.VMEM((n_lt, nt, LANES, t), BF16),
                   pltpu.VMEM((n_heads, nt, t, LANES), F32)] + pipe
        name = "forgetting_attn"
    return pl.pallas_call(
        body,
        grid=(b, d // width),
        in_specs=in_specs,
        out_specs=blk,
        out_shape=jax.ShapeDtypeStruct((b, s, d), BF16),
        scratch_shapes=scratch,
        compiler_params=_cparams("arbitrary", "arbitrary"),
        name=name,
    )(*args)


def _proj_res_kernel(a_ref, w_ref, x_ref, g_ref, gt_ref, o_ref):
    y = _dot(a_ref[0], w_ref[...])
    o_ref[0] = x_ref[0] + gt_ref[0] * (_rms(y) * g_ref[...])


def _proj_res(a, w, x, g, gate):
    b, s, d = x.shape
    kdim = a.shape[-1]
    tm = ROW_TILE
    return pl.pallas_call(
        _proj_res_kernel,
        grid=(b, s // tm),
        in_specs=[
            pl.BlockSpec((1, tm, kdim), lambda i, j: (i, j, 0)),
            pl.BlockSpec((kdim, d), lambda i, j: (0, 0)),
            pl.BlockSpec((1, tm, d), lambda i, j: (i, j, 0)),
            pl.BlockSpec((1, d), lambda i, j: (0, 0)),
            pl.BlockSpec((1, 1, d), lambda i, j: (i, 0, 0)),
        ],
        out_specs=pl.BlockSpec((1, tm, d), lambda i, j: (i, j, 0)),
        out_shape=jax.ShapeDtypeStruct((b, s, d), F32),
        compiler_params=_cparams("arbitrary", "arbitrary"),
        name="proj_norm_residual",
    )(a, w, x, g.reshape(1, d), gate.reshape(b, 1, d))


def _ffn_up_kernel(x_ref, g_ref, sh_ref, sc_ref, wg_ref, wu_ref, wc_ref, bc_ref, o_ref, gbuf_ref):
    tm = x_ref.shape[1]
    halo = SUBLANES

    @pl.when(pl.program_id(1) == 0)
    def _():
        gbuf_ref[0:halo, :] = jnp.zeros((halo, gbuf_ref.shape[1]), F32)

    x = x_ref[0]
    h = ((_rms(x) * g_ref[...]) * (1.0 + sc_ref[0]) + sh_ref[0]).astype(BF16)
    gate = _dot(h, wg_ref[...])
    gbuf_ref[halo:halo + tm, :] = gate
    g1 = gbuf_ref[halo - 1:halo - 1 + tm, :]
    g2 = gbuf_ref[halo - 2:halo - 2 + tm, :]
    cg = bc_ref[...] + wc_ref[0:1, :] * g2 + wc_ref[1:2, :] * g1 + wc_ref[2:3, :] * gate
    act = cg / (1.0 + jnp.exp(-cg))
    up = _dot(h, wu_ref[...])
    o_ref[0] = (act * up).astype(BF16)
    gbuf_ref[0:halo, :] = gbuf_ref[tm:tm + halo, :]


def _ffn_up(x, g, shift, scale, w_gate, w_up, w_conv, b_conv):
    b, s, d = x.shape
    f = w_gate.shape[1]
    tm = FFN_ROW_TILE
    cw = w_conv.shape[0]
    mod_spec = pl.BlockSpec((1, 1, d), lambda i, j: (i, 0, 0))
    return pl.pallas_call(
        _ffn_up_kernel,
        grid=(b, s // tm),
        in_specs=[
            pl.BlockSpec((1, tm, d), lambda i, j: (i, j, 0)),
            pl.BlockSpec((1, d), lambda i, j: (0, 0)),
            mod_spec, mod_spec,
            pl.BlockSpec((d, f), lambda i, j: (0, 0)),
            pl.BlockSpec((d, f), lambda i, j: (0, 0)),
            pl.BlockSpec((cw, f), lambda i, j: (0, 0)),
            pl.BlockSpec((1, f), lambda i, j: (0, 0)),
        ],
        out_specs=pl.BlockSpec((1, tm, f), lambda i, j: (i, j, 0)),
        out_shape=jax.ShapeDtypeStruct((b, s, f), BF16),
        scratch_shapes=[pltpu.VMEM((tm + SUBLANES, f), F32)],
        compiler_params=_cparams("arbitrary", "arbitrary"),
        name="ffn_gate_up_conv",
    )(x, g.reshape(1, d), shift.reshape(b, 1, d), scale.reshape(b, 1, d),
      w_gate, w_up, w_conv, b_conv.reshape(1, f))


def kernel(x, c, w_mod, b_mod, g_mix_pre, g_mix_post, w_qkv, w_o, w_fg, b_fg,
           g_ffn_pre, g_ffn_post, w_ffn_gate, w_ffn_up, w_conv, b_conv, w_ffn_down):
    depth = w_mod.shape[0]
    d = x.shape[-1]
    n_heads = w_fg.shape[-1]
    assert d == N_HEADS * HEAD_DIM and n_heads == N_HEADS and w_conv.shape[1] == 3
    scale = HEAD_DIM ** -0.5

    mod = _modulation(c, w_mod, b_mod)
    w_qkv_b = w_qkv.astype(BF16)
    w_o_b = w_o.astype(BF16)
    w_gate_b = w_ffn_gate.astype(BF16)
    w_up_b = w_ffn_up.astype(BF16)
    w_down_b = w_ffn_down.astype(BF16)
    w_fg_b = jnp.pad(w_fg, ((0, 0), (0, 0), (0, LANES - n_heads))).astype(BF16)
    b_fg_p = jnp.pad(b_fg, ((0, 0), (0, LANES - n_heads))).reshape(-1, 1, LANES)

    for i in range(depth):
        sh_a, sc_a, gt_a, sh_f, sc_f, gt_f = [mod[i, :, j * d:(j + 1) * d] for j in range(N_MOD)]
        if i % 2 == 0:
            q, k, v = _qkv(x, g_mix_pre[i], sh_a, sc_a, w_qkv_b[i], 0.5 * scale)
            o = _attention(q, k, v)
        else:
            j = i // 2
            q, k, v, cum = _qkv(x, g_mix_pre[i], sh_a, sc_a, w_qkv_b[i], scale * LOG2_E,
                                w_fg_b[j], b_fg_p[j])
            o = _attention(q, k, v, cum)
        x = _proj_res(o, w_o_b[i], x, g_mix_post[i], gt_a)
        act = _ffn_up(x, g_ffn_pre[i], sh_f, sc_f, w_gate_b[i], w_up_b[i], w_conv[i], b_conv[i])
        x = _proj_res(act, w_down_b[i], x, g_ffn_post[i], gt_f)
    return x
```

```python
import functools

import jax
import jax.numpy as jnp
from jax import lax
from jax.experimental import pallas as pl
from jax.experimental.pallas import tpu as pltpu

F32 = jnp.float32
BF16 = jnp.bfloat16

N_HEADS = 16
HEAD_DIM = 64
N_MOD = 6
RMS_EPS = 1e-6
LANES = 128
SUBLANES = 8
HEADS_PER_LANE_TILE = LANES // HEAD_DIM
VMEM_LIMIT_BYTES = 56 * 1024 * 1024
NEG_BIG = -1e30
LOG2_E = 1.4426950408889634

ATTN_TILE = 512
ATTN_LANE_TILES = 1
ROW_TILE = 1024
FFN_ROW_TILE = 512
CUMSUM_CHUNK = 256


def _cparams(*sem):
    return pltpu.CompilerParams(dimension_semantics=sem, vmem_limit_bytes=VMEM_LIMIT_BYTES)


def _split3(x):
    p1 = x.astype(BF16)
    r1 = x - p1.astype(F32)
    p2 = r1.astype(BF16)
    r2 = r1 - p2.astype(F32)
    return p1, p2, r2.astype(BF16)


def _dot(a, b):
    return jnp.dot(a, b, preferred_element_type=F32)


def _rms(x):
    return x * lax.rsqrt(jnp.mean(x * x, axis=-1, keepdims=True) + RMS_EPS)


def _mod_kernel(c_ref, w_ref, b_ref, o_ref):
    c = c_ref[...]
    ca = c / (1.0 + jnp.exp(-c))
    c1, c2, _ = _split3(ca)
    w = w_ref[0]
    w1 = w.astype(BF16)
    w2 = (w - w1.astype(F32)).astype(BF16)
    o_ref[0] = _dot(c1, w1) + _dot(c2, w1) + _dot(c1, w2) + b_ref[0]


def _modulation(c, w_mod, b_mod):
    depth, d, m = w_mod.shape
    b = c.shape[0]
    tn = 1536
    return pl.pallas_call(
        _mod_kernel,
        grid=(depth, m // tn),
        in_specs=[
            pl.BlockSpec((b, d), lambda i, j: (0, 0)),
            pl.BlockSpec((1, d, tn), lambda i, j: (i, 0, j)),
            pl.BlockSpec((1, 1, tn), lambda i, j: (i, 0, j)),
        ],
        out_specs=pl.BlockSpec((1, b, tn), lambda i, j: (i, 0, j)),
        out_shape=jax.ShapeDtypeStruct((depth, b, m), F32),
        compiler_params=_cparams("arbitrary", "arbitrary"),
        name="adaln_mod",
    )(c, w_mod, b_mod.reshape(depth, 1, m))


def _qkv_kernel(x_ref, g_ref, sh_ref, sc_ref, w_ref, *rest, d, q_scale, fox):
    if fox:
        wfg_ref, bfg_ref, q_ref, k_ref, v_ref, cum_ref, carry_ref = rest
    else:
        q_ref, k_ref, v_ref = rest
    x = x_ref[0]
    h = (_rms(x) * g_ref[...]) * (1.0 + sc_ref[0]) + sh_ref[0]
    hb = h.astype(BF16)
    q_ref[0] = (_dot(hb, w_ref[:, 0:d]) * q_scale).astype(BF16)
    k_ref[0] = _dot(hb, w_ref[:, d:2 * d]).astype(BF16)
    v_ref[0] = _dot(hb, w_ref[:, 2 * d:3 * d]).astype(BF16)
    if fox:
        tm = x.shape[0]

        @pl.when(pl.program_id(1) == 0)
        def _():
            carry_ref[...] = jnp.zeros_like(carry_ref)

        fl = _dot(hb, wfg_ref[...]) + bfg_ref[...]
        lf = jnp.minimum(fl, 0.0) - jnp.log(1.0 + jnp.exp(-jnp.abs(fl)))
        ch = CUMSUM_CHUNK
        row = lax.broadcasted_iota(jnp.int32, (ch, ch), 0)
        col = lax.broadcasted_iota(jnp.int32, (ch, ch), 1)
        tri = jnp.where(row >= col, 1.0, 0.0).astype(BF16)
        offset = carry_ref[0:1, :]
        for c0 in range(0, tm, ch):
            p1, p2, p3 = _split3(lf[c0:c0 + ch, :])
            cs = _dot(tri, p1) + _dot(tri, p2) + _dot(tri, p3) + offset
            cum_ref[0, c0:c0 + ch, :] = cs
            offset = cs[ch - 1:ch, :]
        carry_ref[...] = jnp.broadcast_to(offset, carry_ref.shape)


def _qkv(x, g, shift, scale, w_qkv, q_scale, w_fg=None, b_fg=None):
    b, s, d = x.shape
    tm = ROW_TILE
    fox = w_fg is not None
    row_spec = pl.BlockSpec((1, tm, d), lambda i, j: (i, j, 0))
    vec_spec = pl.BlockSpec((1, d), lambda i, j: (0, 0))
    mod_spec = pl.BlockSpec((1, 1, d), lambda i, j: (i, 0, 0))
    in_specs = [row_spec, vec_spec, mod_spec, mod_spec,
                pl.BlockSpec((d, 3 * d), lambda i, j: (0, 0))]
    args = [x, g.reshape(1, d), shift.reshape(b, 1, d), scale.reshape(b, 1, d), w_qkv]
    out_specs = [row_spec, row_spec, row_spec]
    out_shape = [jax.ShapeDtypeStruct((b, s, d), BF16)] * 3
    scratch = []
    if fox:
        in_specs += [pl.BlockSpec((d, LANES), lambda i, j: (0, 0)),
                     pl.BlockSpec((1, LANES), lambda i, j: (0, 0))]
        args += [w_fg, b_fg]
        out_specs.append(pl.BlockSpec((1, tm, LANES), lambda i, j: (i, j, 0)))
        out_shape.append(jax.ShapeDtypeStruct((b, s, LANES), F32))
        scratch.append(pltpu.VMEM((SUBLANES, LANES), F32))
    return pl.pallas_call(
        functools.partial(_qkv_kernel, d=d, q_scale=q_scale, fox=fox),
        grid=(b, s // tm),
        in_specs=in_specs,
        out_specs=out_specs,
        out_shape=out_shape,
        scratch_shapes=scratch,
        compiler_params=_cparams("arbitrary", "arbitrary"),
        name="norm_qkv_fox" if fox else "norm_qkv",
    )(*args)


def _sublane_suffix_excl_prod(tot):
    n = SUBLANES
    rid = lax.broadcasted_iota(jnp.int32, tot.shape, 0)
    x = jnp.where(rid < n - 1, pltpu.roll(tot, n - 1, 0), 1.0)
    y = x * jnp.where(rid < n - 1, pltpu.roll(x, n - 1, 0), 1.0)
    y = y * jnp.where(rid < n - 2, pltpu.roll(y, n - 2, 0), 1.0)
    y = y * jnp.where(rid < n - 4, pltpu.roll(y, n - 4, 0), 1.0)
    return y


def _store_head_queries(qt_ref, g0, i, q_tile):
    qt = q_tile.astype(F32).T
    row = lax.broadcasted_iota(jnp.int32, qt.shape, 0)
    for h in range(HEADS_PER_LANE_TILE):
        mine = (row >= h * HEAD_DIM) & (row < (h + 1) * HEAD_DIM)
        qt_ref[g0 + h, i] = jnp.where(mine, qt, 0.0).astype(BF16)


def _lane_tile(p):
    return pl.ds(p * LANES, LANES)


def _head_rows(g):
    return pl.ds((g % HEADS_PER_LANE_TILE) * HEAD_DIM, HEAD_DIM)


def _advance(qi, kj):
    last = kj == 0
    return jnp.where(last, qi + 1, qi), jnp.where(last, qi + 1, kj - 1)


def _walk_tiles(nt, first_step, later_step, finish, state0):
    n_steps = nt * (nt + 1) // 2
    zero = jnp.int32(0)

    def clamp(p):
        return jnp.minimum(p[0], nt - 1), jnp.minimum(p[1], nt - 1)

    cur = (zero, zero)
    nxt = _advance(*cur)
    state = first_step(None, cur, clamp(nxt), state0)

    def body(_, c):
        prev, cur, state = (c[0], c[1]), (c[2], c[3]), c[4]
        nxt = _advance(*cur)
        state = lax.cond(cur[1] == cur[0],
                         lambda st: first_step(prev, cur, clamp(nxt), st),
                         lambda st: later_step(prev, cur, clamp(nxt), st), state)
        return (*cur, *nxt, state)

    c = lax.fori_loop(1, n_steps, body, (*cur, *nxt, state))
    finish((c[0], c[1]), c[4])


def _sb_kernel(q_ref, k_ref, v_ref, o_ref, kp_ref, vt_ref, tmp_ref, qt_ref, z_ref, w_ref, a_ref,
               acc_ref, *, t):
    s = q_ref.shape[1]
    nt = s // t
    m_rows = t // SUBLANES
    lane_tiles = range(q_ref.shape[2] // LANES)
    heads = range(len(lane_tiles) * HEADS_PER_LANE_TILE)

    def prep(i, carry):
        base = pl.multiple_of(i * t, t)
        for p in lane_tiles:
            for r in range(SUBLANES):
                tmp_ref[pl.ds(r, m_rows, stride=SUBLANES), :] = (
                    k_ref[0, pl.ds(base + r * m_rows, m_rows), _lane_tile(p)].astype(F32))
            kp_ref[p, i] = tmp_ref[...].astype(BF16)
            for r in range(SUBLANES):
                tmp_ref[pl.ds(r, m_rows, stride=SUBLANES), :] = (
                    v_ref[0, pl.ds(base + r * m_rows, m_rows), _lane_tile(p)].astype(F32))
            vt_ref[p, i] = tmp_ref[...].T.astype(BF16)
            _store_head_queries(qt_ref, p * HEADS_PER_LANE_TILE, i,
                                q_ref[0, pl.ds(base, t), _lane_tile(p)])
        return carry

    lax.fori_loop(0, nt, prep, 0)

    rho = lax.broadcasted_iota(jnp.int32, (t, t), 0)
    key_in_tile = (rho & (SUBLANES - 1)) * m_rows + (rho >> 3)
    qry_in_tile = lax.broadcasted_iota(jnp.int32, (t, t), 1)
    strict = key_in_tile < qry_in_tile
    ones = jnp.ones((SUBLANES, t), F32)

    def scores(pos):
        return [_dot(kp_ref[g // HEADS_PER_LANE_TILE, pos[1]], qt_ref[g, pos[0]]) for g in heads]

    def values(pos):
        return jnp.concatenate(
            [_dot(vt_ref[g // HEADS_PER_LANE_TILE, pos[1], _head_rows(g), :], a_ref[g])
             for g in heads], axis=0)

    def weights(g, carry_p, mask):
        omb = 0.5 - 0.5 * jnp.tanh(z_ref[g])
        if mask is not None:
            omb = jnp.where(mask, omb, 1.0)
        w_ref[g] = omb
        parts = [omb[m * SUBLANES:(m + 1) * SUBLANES, :] for m in range(m_rows)]
        while len(parts) > 1:
            parts = [parts[i] * parts[i + 1] for i in range(0, len(parts), 2)]
        run = _sublane_suffix_excl_prod(parts[0]) * carry_p
        pack = 2 * SUBLANES
        for j in reversed(range(t // pack)):
            w = w_ref[g, pl.ds(j * pack, pack), :]
            mid = run * w[SUBLANES:, :]
            low = mid * w[:SUBLANES, :]
            a_ref[g, pl.ds(j * pack, pack), :] = jnp.concatenate(
                [mid - low, run - mid], axis=0).astype(BF16)
            run = low
        return jnp.broadcast_to(run[0:1, :], (SUBLANES, t))

    def write_out(pos, out):
        rows = pl.ds(pl.multiple_of(pos[0] * t, t), t)
        for p in lane_tiles:
            o_ref[0, rows, _lane_tile(p)] = out[p * LANES:(p + 1) * LANES, :].T.astype(BF16)

    def first_step(prev, cur, nxt, state):
        if prev is None:
            zs = scores(cur)
            for g in heads:
                z_ref[g] = zs[g]
        zs = scores(nxt)
        if prev is not None:
            write_out(prev, acc_ref[...] + values(prev))
        acc_ref[...] = jnp.zeros_like(acc_ref)
        state = tuple(weights(g, ones, strict) for g in heads)
        for g in heads:
            z_ref[g] = zs[g]
        return state

    def later_step(prev, cur, nxt, state):
        zs = scores(nxt)
        acc_ref[...] += values(prev)
        state = tuple(weights(g, state[g], None) for g in heads)
        for g in heads:
            z_ref[g] = zs[g]
        return state

    def finish(prev, state):
        write_out(prev, acc_ref[...] + values(prev))

    _walk_tiles(nt, first_step, later_step, finish, tuple(ones for _ in heads))


def _fox_kernel(q_ref, k_ref, v_ref, cum_ref, o_ref, vt_ref, bias_ref, qt_ref, z_ref, w_ref, p_ref,
                acc_ref, *, t):
    s = q_ref.shape[1]
    nt = s // t
    lane_tiles = range(q_ref.shape[2] // LANES)
    heads = range(len(lane_tiles) * HEADS_PER_LANE_TILE)
    head0 = pl.program_id(1) * len(heads)

    def prep(i, carry):
        base = pl.multiple_of(i * t, t)
        for p in lane_tiles:
            vt_ref[p, i] = v_ref[0, pl.ds(base, t), _lane_tile(p)].astype(F32).T.astype(BF16)
            _store_head_queries(qt_ref, p * HEADS_PER_LANE_TILE, i,
                                q_ref[0, pl.ds(base, t), _lane_tile(p)])
        cum = cum_ref[0, pl.ds(base, t), :]
        lane = lax.broadcasted_iota(jnp.int32, cum.shape, 1)
        for g in heads:
            col = jnp.sum(jnp.where(lane == head0 + g, cum, 0.0), axis=1, keepdims=True)
            bias_ref[g, i] = jnp.broadcast_to(col * LOG2_E, (t, LANES))
        return carry

    lax.fori_loop(0, nt, prep, 0)

    key_in_tile = lax.broadcasted_iota(jnp.int32, (t, t), 0)
    qry_in_tile = lax.broadcasted_iota(jnp.int32, (t, t), 1)
    causal = key_in_tile <= qry_in_tile
    rep = t // LANES
    fresh = (jnp.full((1, t), NEG_BIG, F32), jnp.zeros((1, t), F32), jnp.zeros((1, t), F32))
    reread = jnp.minimum(pl.program_id(0), 0)

    def scores(pos):
        rows = pl.ds(pl.multiple_of(pos[1] * t, t), t)
        return [_dot(k_ref[0, rows, _lane_tile(g // HEADS_PER_LANE_TILE)], qt_ref[g, pos[0]])
                for g in heads]

    def values(pos):
        return [_dot(vt_ref[g // HEADS_PER_LANE_TILE, pos[1], _head_rows(g), :], p_ref[g])
                for g in heads]

    def weights(g, pos, st, mask):
        m_run, l_run, _ = st
        z = z_ref[g] - jnp.concatenate([bias_ref[g, pos[1]]] * rep, axis=1)
        if mask is not None:
            z = jnp.where(mask, z, NEG_BIG)
        w_ref[g] = z
        m_new = jnp.maximum(m_run, jnp.max(z, axis=0, keepdims=True))
        p = jnp.exp2(w_ref[g + reread] - m_new)
        alpha = jnp.exp2(m_run - m_new)
        p_ref[g] = p.astype(BF16)
        return m_new, alpha * l_run + jnp.sum(p, axis=0, keepdims=True), alpha

    def rescaled(g, av, st):
        return acc_ref[pl.ds(g * HEAD_DIM, HEAD_DIM), :] * st[2] + av[g]

    def write_out(pos, av, state):
        rows = pl.ds(pl.multiple_of(pos[0] * t, t), t)
        for p in lane_tiles:
            out = jnp.concatenate(
                [rescaled(g, av, state[g]) / state[g][1]
                 for g in range(p * HEADS_PER_LANE_TILE, (p + 1) * HEADS_PER_LANE_TILE)], axis=0)
            o_ref[0, rows, _lane_tile(p)] = out.T.astype(BF16)

    def first_step(prev, cur, nxt, state):
        if prev is None:
            zs = scores(cur)
            for g in heads:
                z_ref[g] = zs[g]
        zs = scores(nxt)
        if prev is not None:
            write_out(prev, values(prev), state)
        acc_ref[...] = jnp.zeros_like(acc_ref)
        state = tuple(weights(g, cur, fresh, causal) for g in heads)
        for g in heads:
            z_ref[g] = zs[g]
        return state

    def later_step(prev, cur, nxt, state):
        zs = scores(nxt)
        av = values(prev)
        for g in heads:
            acc_ref[pl.ds(g * HEAD_DIM, HEAD_DIM), :] = rescaled(g, av, state[g])
        state = tuple(weights(g, cur, state[g], None) for g in heads)
        for g in heads:
            z_ref[g] = zs[g]
        return state

    def finish(prev, state):
        write_out(prev, values(prev), state)

    _walk_tiles(nt, first_step, later_step, finish, tuple(fresh for _ in heads))


def _attention(q, k, v, cum=None):
    b, s, d = q.shape
    t = ATTN_TILE
    nt = s // t
    n_lt = ATTN_LANE_TILES
    n_heads = n_lt * HEADS_PER_LANE_TILE
    width = n_lt * LANES
    blk = pl.BlockSpec((1, s, width), lambda i, j: (i, 0, j))
    in_specs = [blk, blk, blk]
    args = [q, k, v]
    pipe = [pltpu.VMEM((n_heads, nt, LANES, t), BF16),
            pltpu.VMEM((n_heads, t, t), F32), pltpu.VMEM((n_heads, t, t), F32),
            pltpu.VMEM((n_heads, t, t), BF16), pltpu.VMEM((width, t), F32)]
    if cum is None:
        body = functools.partial(_sb_kernel, t=t)
        scratch = [pltpu.VMEM((n_lt, nt, t, LANES), BF16), pltpu.VMEM((n_lt, nt, LANES, t), BF16),
                   pltpu.VMEM((t, LANES), F32)] + pipe
        name = "stickbreak_attn"
    else:
        body = functools.partial(_fox_kernel, t=t)
        in_specs.append(pl.BlockSpec((1, s, LANES), lambda i, j: (i, 0, 0)))
        args.append(cum)
        scratch = [pltpu.VMEM((n_lt, nt, LANES, t), BF16),
                   pltpu.VMEM((n_heads, nt, t, LANES), F32)] + pipe
        name = "forgetting_attn"
    return pl.pallas_call(
        body,
        grid=(b, d // width),
        in_specs=in_specs,
        out_specs=blk,
        out_shape=jax.ShapeDtypeStruct((b, s, d), BF16),
        scratch_shapes=scratch,
        compiler_params=_cparams("arbitrary", "arbitrary"),
        name=name,
    )(*args)


def _proj_res_kernel(a_ref, w_ref, x_ref, g_ref, gt_ref, o_ref):
    y = _dot(a_ref[0], w_ref[...])
    o_ref[0] = x_ref[0] + gt_ref[0] * (_rms(y) * g_ref[...])


def _proj_res(a, w, x, g, gate):
    b, s, d = x.shape
    kdim = a.shape[-1]
    tm = ROW_TILE
    return pl.pallas_call(
        _proj_res_kernel,
        grid=(b, s // tm),
        in_specs=[
            pl.BlockSpec((1, tm, kdim), lambda i, j: (i, j, 0)),
            pl.BlockSpec((kdim, d), lambda i, j: (0, 0)),
            pl.BlockSpec((1, tm, d), lambda i, j: (i, j, 0)),
            pl.BlockSpec((1, d), lambda i, j: (0, 0)),
            pl.BlockSpec((1, 1, d), lambda i, j: (i, 0, 0)),
        ],
        out_specs=pl.BlockSpec((1, tm, d), lambda i, j: (i, j, 0)),
        out_shape=jax.ShapeDtypeStruct((b, s, d), F32),
        compiler_params=_cparams("arbitrary", "arbitrary"),
        name="proj_norm_residual",
    )(a, w, x, g.reshape(1, d), gate.reshape(b, 1, d))


def _ffn_up_kernel(a_ref, wo_ref, ga_ref, gta_ref, x_ref, g_ref, sh_ref, sc_ref, wg_ref, wu_ref,
                   wc_ref, bc_ref, xo_ref, o_ref, gbuf_ref):
    tm = x_ref.shape[1]
    halo = SUBLANES

    @pl.when(pl.program_id(1) == 0)
    def _():
        gbuf_ref[0:halo, :] = jnp.zeros((halo, gbuf_ref.shape[1]), F32)

    y = _dot(a_ref[0], wo_ref[...])
    x = x_ref[0] + gta_ref[0] * (_rms(y) * ga_ref[...])
    xo_ref[0] = x
    h = ((_rms(x) * g_ref[...]) * (1.0 + sc_ref[0]) + sh_ref[0]).astype(BF16)
    gate = _dot(h, wg_ref[...])
    gbuf_ref[halo:halo + tm, :] = gate
    g1 = gbuf_ref[halo - 1:halo - 1 + tm, :]
    g2 = gbuf_ref[halo - 2:halo - 2 + tm, :]
    cg = bc_ref[...] + wc_ref[0:1, :] * g2 + wc_ref[1:2, :] * g1 + wc_ref[2:3, :] * gate
    act = cg / (1.0 + jnp.exp(-cg))
    up = _dot(h, wu_ref[...])
    o_ref[0] = (act * up).astype(BF16)
    gbuf_ref[0:halo, :] = gbuf_ref[tm:tm + halo, :]


def _ffn_up(a, w_o, g_post, gate_a, x, g, shift, scale, w_gate, w_up, w_conv, b_conv):
    b, s, d = x.shape
    f = w_gate.shape[1]
    tm = FFN_ROW_TILE
    cw = w_conv.shape[0]
    mod_spec = pl.BlockSpec((1, 1, d), lambda i, j: (i, 0, 0))
    row_spec = pl.BlockSpec((1, tm, d), lambda i, j: (i, j, 0))
    vec_spec = pl.BlockSpec((1, d), lambda i, j: (0, 0))
    return pl.pallas_call(
        _ffn_up_kernel,
        grid=(b, s // tm),
        in_specs=[
            row_spec,
            pl.BlockSpec((d, d), lambda i, j: (0, 0)),
            vec_spec, mod_spec,
            row_spec,
            vec_spec,
            mod_spec, mod_spec,
            pl.BlockSpec((d, f), lambda i, j: (0, 0)),
            pl.BlockSpec((d, f), lambda i, j: (0, 0)),
            pl.BlockSpec((cw, f), lambda i, j: (0, 0)),
            pl.BlockSpec((1, f), lambda i, j: (0, 0)),
        ],
        out_specs=[row_spec, pl.BlockSpec((1, tm, f), lambda i, j: (i, j, 0))],
        out_shape=[jax.ShapeDtypeStruct((b, s, d), F32), jax.ShapeDtypeStruct((b, s, f), BF16)],
        scratch_shapes=[pltpu.VMEM((tm + SUBLANES, f), F32)],
        compiler_params=_cparams("arbitrary", "arbitrary"),
        name="proj_ffn_gate_up_conv",
    )(a, w_o, g_post.reshape(1, d), gate_a.reshape(b, 1, d),
      x, g.reshape(1, d), shift.reshape(b, 1, d), scale.reshape(b, 1, d),
      w_gate, w_up, w_conv, b_conv.reshape(1, f))


def kernel(x, c, w_mod, b_mod, g_mix_pre, g_mix_post, w_qkv, w_o, w_fg, b_fg,
           g_ffn_pre, g_ffn_post, w_ffn_gate, w_ffn_up, w_conv, b_conv, w_ffn_down):
    depth = w_mod.shape[0]
    d = x.shape[-1]
    n_heads = w_fg.shape[-1]
    assert d == N_HEADS * HEAD_DIM and n_heads == N_HEADS and w_conv.shape[1] == 3
    scale = HEAD_DIM ** -0.5

    mod = _modulation(c, w_mod, b_mod)
    w_qkv_b = w_qkv.astype(BF16)
    w_o_b = w_o.astype(BF16)
    w_gate_b = w_ffn_gate.astype(BF16)
    w_up_b = w_ffn_up.astype(BF16)
    w_down_b = w_ffn_down.astype(BF16)
    w_fg_b = jnp.pad(w_fg, ((0, 0), (0, 0), (0, LANES - n_heads))).astype(BF16)
    b_fg_p = jnp.pad(b_fg, ((0, 0), (0, LANES - n_heads))).reshape(-1, 1, LANES)

    for i in range(depth):
        sh_a, sc_a, gt_a, sh_f, sc_f, gt_f = [mod[i, :, j * d:(j + 1) * d] for j in range(N_MOD)]
        if i % 2 == 0:
            q, k, v = _qkv(x, g_mix_pre[i], sh_a, sc_a, w_qkv_b[i], 0.5 * scale)
            o = _attention(q, k, v)
        else:
            j = i // 2
            q, k, v, cum = _qkv(x, g_mix_pre[i], sh_a, sc_a, w_qkv_b[i], scale * LOG2_E,
                                w_fg_b[j], b_fg_p[j])
            o = _attention(q, k, v, cum)
        x, act = _ffn_up(o, w_o_b[i], g_mix_post[i], gt_a, x, g_ffn_pre[i], sh_f, sc_f,
                         w_gate_b[i], w_up_b[i], w_conv[i], b_conv[i])
        x = _proj_res(act, w_down_b[i], x, g_ffn_post[i], gt_f)
    return x
```
